```python
import math
import jax, jax.numpy as jnp
from jax import lax
import numpy as np

D_MODEL = 4096
BATCH = 4
SEQ = 2048
DEPTH = 1

EPS = 1e-6

MLA_HEADS = 16
QK_NOPE_DIM = 128
QK_ROPE_DIM = 64
QK_HEAD_DIM = QK_NOPE_DIM + QK_ROPE_DIM
V_HEAD_DIM = 128
Q_LORA_RANK = 1024
KV_LORA_RANK = 512
ROPE_BASE = 10000.0
Q_BLOCK = 128
MLA_WIDTH = MLA_HEADS * V_HEAD_DIM

SSM_WIDTH = D_MODEL - MLA_WIDTH
SSM_HEAD_DIM = 64
SSM_HEADS = SSM_WIDTH // SSM_HEAD_DIM
SSM_GROUPS = 8
SSM_HEADS_PER_GROUP = SSM_HEADS // SSM_GROUPS
D_STATE = 128
D_CONV = 5
CHUNK = 128
CONV_DIM = SSM_WIDTH + 2 * SSM_GROUPS * D_STATE

MIX_WIDTH = MLA_WIDTH + SSM_WIDTH

IN_SIZES = (Q_LORA_RANK, KV_LORA_RANK, QK_ROPE_DIM, SSM_WIDTH, CONV_DIM, SSM_HEADS, SSM_HEADS)
IN_WIDTH = sum(IN_SIZES)
IN_SPLITS = tuple(int(v) for v in np.cumsum(IN_SIZES)[:-1])

PEER_HEADS = 8
N_KEYS = 128
N_EXPERTS = N_KEYS * N_KEYS
PEER_TOPK = 16
PEER_QDIM = 256
PEER_TOKEN_BLOCK = 128

kernel_name = "hymba_mla_ssd_peer_encoder_layer"


def rms_norm(x, gain):
    xf = x.astype(jnp.float32)
    y = xf * lax.rsqrt(jnp.mean(xf * xf, axis=-1, keepdims=True) + EPS)
    return (y * gain.astype(jnp.float32)).astype(x.dtype)


def apply_rope(t, positions):
    half = QK_ROPE_DIM // 2
    inv_freq = ROPE_BASE ** (-jnp.arange(half, dtype=jnp.float32) / half)
    ang = positions.astype(jnp.float32)[..., None] * inv_freq
    cos = jnp.cos(ang)[:, :, None, :]
    sin = jnp.sin(ang)[:, :, None, :]
    tf = t.astype(jnp.float32)
    t1, t2 = tf[..., :half], tf[..., half:]
    return jnp.concatenate([t1 * cos - t2 * sin, t1 * sin + t2 * cos], axis=-1).astype(t.dtype)


def mla_mixer(c_q, c_kv, k_rope, positions, q_a_norm, w_uq, kv_a_norm, w_ukv, q_norm, k_norm, attn_out_norm):
    b, s, _ = c_q.shape
    q = (rms_norm(c_q, q_a_norm) @ w_uq).reshape(b, s, MLA_HEADS, QK_HEAD_DIM)
    kv = (rms_norm(c_kv, kv_a_norm) @ w_ukv).reshape(b, s, MLA_HEADS, QK_NOPE_DIM + V_HEAD_DIM)
    k_nope, v = kv[..., :QK_NOPE_DIM], kv[..., QK_NOPE_DIM:]
    k = jnp.concatenate(
        [k_nope, jnp.broadcast_to(k_rope[:, :, None, :], (b, s, MLA_HEADS, QK_ROPE_DIM))], axis=-1)
    q = rms_norm(q, q_norm)
    k = rms_norm(k, k_norm)
    q = jnp.concatenate([q[..., :QK_NOPE_DIM], apply_rope(q[..., QK_NOPE_DIM:], positions)], axis=-1)
    k = jnp.concatenate([k[..., :QK_NOPE_DIM], apply_rope(k[..., QK_NOPE_DIM:], positions)], axis=-1)
    scale = QK_HEAD_DIM ** -0.5
    n_blk = s // Q_BLOCK
    q_blocks = jnp.moveaxis(q.reshape(b, n_blk, Q_BLOCK, MLA_HEADS, QK_HEAD_DIM), 1, 0)

    def attend(q_blk):
        sc = jnp.einsum("bqhd,bkhd->bhqk", q_blk, k, preferred_element_type=jnp.float32) * scale
        p = jax.nn.softmax(sc, axis=-1)
        return jnp.einsum("bhqk,bkhd->bqhd", p.astype(v.dtype), v)

    o = lax.map(attend, q_blocks)
    o = jnp.moveaxis(o, 0, 1).reshape(b, s, MLA_HEADS, V_HEAD_DIM)
    o = rms_norm(o, attn_out_norm)
    return o.reshape(b, s, MLA_WIDTH)


def ssd_scan(xs, dt_raw, a_log, dt_bias, bs, cs):
    b, s, g, r, p = xs.shape
    nc = s // CHUNK
    dt = jax.nn.softplus(dt_raw.astype(jnp.float32) + dt_bias.astype(jnp.float32)).reshape(b, s, g, r)
    a = -jnp.exp(a_log.astype(jnp.float32)).reshape(g, r)
    a_dt = (dt * a).reshape(b, nc, CHUNK, g, r).transpose(0, 3, 4, 1, 2)
    xdt = (xs.astype(jnp.float32) * dt[..., None]).reshape(b, nc, CHUNK, g, r, p)
    bc = bs.astype(jnp.float32).reshape(b, nc, CHUNK, g, D_STATE)
    cc = cs.astype(jnp.float32).reshape(b, nc, CHUNK, g, D_STATE)
    a_cum = jnp.cumsum(a_dt, axis=-1)
    seg = a_cum[..., :, None] - a_cum[..., None, :]
    mask = jnp.tril(jnp.ones((CHUNK, CHUNK), dtype=bool))
    l_mat = jnp.exp(jnp.where(mask, seg, -jnp.inf))
    y_diag = jnp.einsum("bclgn,bcsgn,bgrcls,bcsgrp->bclgrp", cc, bc, l_mat, xdt)
    decay_states = jnp.exp(a_cum[..., -1:] - a_cum)
    states = jnp.einsum("bclgn,bgrcl,bclgrp->bcgrpn", bc, decay_states, xdt)
    chunk_decay = jnp.exp(a_cum[..., -1])

    def carry_state(h, inp):
        s_c, d_c = inp
        return h * d_c[..., None, None] + s_c, h

    h0 = jnp.zeros((b, g, r, p, D_STATE), jnp.float32)
    _, states_in = lax.scan(carry_state, h0, (jnp.moveaxis(states, 1, 0), jnp.moveaxis(chunk_decay, -1, 0)))
    states_in = jnp.moveaxis(states_in, 0, 1)
    y_off = jnp.einsum("bclgn,bcgrpn,bgrcl->bclgrp", cc, states_in, jnp.exp(a_cum))
    return (y_diag + y_off).reshape(b, s, g, r, p)


def ssd_mixer(z, xbc, dt_f, dt_b, conv_w, conv_b, a_log_fwd, a_log_bwd, dt_bias_fwd, dt_bias_bwd,
              d_skip, ssm_out_norm):
    b, s, _ = z.shape
    xbc = lax.conv_general_dilated(
        xbc, conv_w, window_strides=(1,), padding=((D_CONV // 2, D_CONV // 2),),
        dimension_numbers=("NWC", "WIO", "NWC"), feature_group_count=CONV_DIM)
    xbc = jax.nn.silu(xbc + conv_b)
    gn = SSM_GROUPS * D_STATE
    xs = xbc[..., :SSM_WIDTH].reshape(b, s, SSM_GROUPS, SSM_HEADS_PER_GROUP, SSM_HEAD_DIM)
    bs = xbc[..., SSM_WIDTH:SSM_WIDTH + gn].reshape(b, s, SSM_GROUPS, D_STATE)
    cs = xbc[..., SSM_WIDTH + gn:].reshape(b, s, SSM_GROUPS, D_STATE)
    y_fwd = ssd_scan(xs, dt_f, a_log_fwd, dt_bias_fwd, bs, cs)
    y_bwd = jnp.flip(ssd_scan(jnp.flip(xs, 1), jnp.flip(dt_b, 1), a_log_bwd, dt_bias_bwd,
                              jnp.flip(bs, 1), jnp.flip(cs, 1)), 1)
    d = d_skip.astype(jnp.float32).reshape(SSM_GROUPS, SSM_HEADS_PER_GROUP)[..., None]
    y = y_fwd + y_bwd + d * xs.astype(jnp.float32)
    y = y.reshape(b, s, SSM_WIDTH) * jax.nn.silu(z.astype(jnp.float32))
    y = rms_norm(y.reshape(b, s, SSM_GROUPS, SSM_WIDTH // SSM_GROUPS),
                 ssm_out_norm.reshape(SSM_GROUPS, SSM_WIDTH // SSM_GROUPS))
    return y.reshape(b, s, SSM_WIDTH).astype(z.dtype)


def peer_ffn(xn, w_query, sub_keys, expert_u, expert_v):
    b, s, d = xn.shape
    q = (xn @ w_query).reshape(b, s, PEER_HEADS, 2, PEER_QDIM // 2)
    sc = jnp.einsum("bshcd,hckd->bshck", q, sub_keys, preferred_element_type=jnp.float32)
    v1, i1 = lax.top_k(sc[..., 0, :], PEER_TOPK)
    v2, i2 = lax.top_k(sc[..., 1, :], PEER_TOPK)
    cand = (v1[..., :, None] + v2[..., None, :]).reshape(b, s, PEER_HEADS, PEER_TOPK * PEER_TOPK)
    top_s, top_c = lax.top_k(cand, PEER_TOPK)
    e1 = jnp.take_along_axis(i1, top_c // PEER_TOPK, axis=-1)
    e2 = jnp.take_along_axis(i2, top_c % PEER_TOPK, axis=-1)
    idx = (e1 * N_KEYS + e2).reshape(b * s, PEER_HEADS * PEER_TOPK)
    gates = jax.nn.softmax(top_s, axis=-1).reshape(b * s, PEER_HEADS * PEER_TOPK)
    n_blk = (b * s) // PEER_TOKEN_BLOCK
    xb = xn.reshape(n_blk, PEER_TOKEN_BLOCK, d)
    idx_b = idx.reshape(n_blk, PEER_TOKEN_BLOCK, PEER_HEADS * PEER_TOPK)
    g_b = gates.reshape(n_blk, PEER_TOKEN_BLOCK, PEER_HEADS * PEER_TOPK)

    def block(args):
        xt, it, gt = args
        u = expert_u[it]
        v = expert_v[it]
        act = jax.nn.gelu(jnp.einsum("td,tkd->tk", xt, u, preferred_element_type=jnp.float32))
        return jnp.einsum("tk,tkd->td", (gt * act).astype(v.dtype), v)

    out = lax.map(block, (xb, idx_b, g_b))
    return out.reshape(b, s, d)


def setup_inputs(seed: int = 0) -> dict:
    key = jax.random.key(seed)
    ks = jax.random.split(key, 26)
    L = DEPTH

    def normal(k, shape, scale):
        return jax.random.normal(k, shape, jnp.float32) * scale

    def gain(k, shape):
        return 1.0 + 0.01 * jax.random.normal(k, shape, jnp.float32)

    def dt_bias(k):
        dt = jnp.exp(jax.random.uniform(k, (L, SSM_HEADS), jnp.float32,
                                        minval=math.log(1e-3), maxval=math.log(1e-1)))
        return dt + jnp.log(-jnp.expm1(-dt))

    x = normal(ks[0], (BATCH, SEQ, D_MODEL), 1.0)
    positions = (jax.random.randint(ks[1], (BATCH, 1), 0, 4096, dtype=jnp.int32)
                 + jnp.arange(SEQ, dtype=jnp.int32)[None, :])
    return {
        "x": x,
        "positions": positions,
        "norm_mix": gain(ks[2], (L, D_MODEL)),
        "w_in": normal(ks[3], (L, D_MODEL, IN_WIDTH), D_MODEL ** -0.5),
        "q_a_norm": gain(ks[4], (L, Q_LORA_RANK)),
        "w_uq": normal(ks[5], (L, Q_LORA_RANK, MLA_HEADS * QK_HEAD_DIM), Q_LORA_RANK ** -0.5),
        "kv_a_norm": gain(ks[6], (L, KV_LORA_RANK)),
        "w_ukv": normal(ks[7], (L, KV_LORA_RANK, MLA_HEADS * (QK_NOPE_DIM + V_HEAD_DIM)), KV_LORA_RANK ** -0.5),
        "q_norm": gain(ks[8], (L, QK_HEAD_DIM)),
        "k_norm": gain(ks[9], (L, QK_HEAD_DIM)),
        "attn_out_norm": gain(ks[10], (L, MLA_HEADS, V_HEAD_DIM)),
        "conv_w": normal(ks[11], (L, D_CONV, 1, CONV_DIM), D_CONV ** -0.5),
        "conv_b": normal(ks[12], (L, CONV_DIM), 0.01),
        "a_log_fwd": jnp.log(jax.random.uniform(ks[13], (L, SSM_HEADS), jnp.float32, minval=1.0, maxval=16.0)),
        "a_log_bwd": jnp.log(jax.random.uniform(ks[14], (L, SSM_HEADS), jnp.float32, minval=1.0, maxval=16.0)),
        "dt_bias_fwd": dt_bias(ks[15]),
        "dt_bias_bwd": dt_bias(ks[16]),
        "d_skip": gain(ks[17], (L, SSM_HEADS)),
        "ssm_out_norm": gain(ks[18], (L, SSM_WIDTH)),
        "w_out": normal(ks[19], (L, MIX_WIDTH, D_MODEL), MIX_WIDTH ** -0.5),
        "norm_ffn": gain(ks[20], (L, D_MODEL)),
        "w_query": normal(ks[21], (L, D_MODEL, PEER_HEADS * PEER_QDIM), D_MODEL ** -0.5),
        "sub_keys": normal(ks[22], (L, PEER_HEADS, 2, N_KEYS, PEER_QDIM // 2), (PEER_QDIM // 2) ** -0.5),
        "expert_u": normal(ks[23], (L, N_EXPERTS, D_MODEL), D_MODEL ** -0.5),
        "expert_v": normal(ks[24], (L, N_EXPERTS, D_MODEL), PEER_TOPK ** -0.5),
    }


def reference(x, positions, norm_mix, w_in, q_a_norm, w_uq, kv_a_norm, w_ukv, q_norm, k_norm,
              attn_out_norm, conv_w, conv_b, a_log_fwd, a_log_bwd, dt_bias_fwd, dt_bias_bwd, d_skip,
              ssm_out_norm, w_out, norm_ffn, w_query, sub_keys, expert_u, expert_v):
    h = x
    for l in range(DEPTH):
        xn = rms_norm(h, norm_mix[l])
        proj = xn @ w_in[l]
        c_q, c_kv, k_rope, z, xbc, dt_f, dt_b = jnp.split(proj, IN_SPLITS, axis=-1)
        attn = mla_mixer(c_q, c_kv, k_rope, positions, q_a_norm[l], w_uq[l], kv_a_norm[l], w_ukv[l],
                         q_norm[l], k_norm[l], attn_out_norm[l])
        ssm = ssd_mixer(z, xbc, dt_f, dt_b, conv_w[l], conv_b[l], a_log_fwd[l], a_log_bwd[l],
                        dt_bias_fwd[l], dt_bias_bwd[l], d_skip[l], ssm_out_norm[l])
        h = h + jnp.concatenate([attn, ssm.astype(attn.dtype)], axis=-1) @ w_out[l]
        hn = rms_norm(h, norm_ffn[l])
        h = h + peer_ffn(hn, w_query[l], sub_keys[l], expert_u[l], expert_v[l]).astype(h.dtype)
    return h.astype(x.dtype)
```

```python
import functools

import jax
import jax.numpy as jnp
from jax import lax
from jax.experimental import pallas as pl
from jax.experimental.pallas import tpu as pltpu

F32 = jnp.float32
BF16 = jnp.bfloat16

EPS = 1e-6
ROPE_BASE = 10000.0
QK_NOPE_DIM = 128
QK_ROPE_DIM = 64
QK_HEAD_DIM = QK_NOPE_DIM + QK_ROPE_DIM
QK_PAD_DIM = 256
V_HEAD_DIM = 128
SSM_HEAD_DIM = 64
D_STATE = 128
D_CONV = 5
CHUNK = 128
PEER_TOPK = 16
LANES = 128
HALO = 16
VMEM_LIMIT = 56 * 1024 * 1024


def _params(sem, vmem=VMEM_LIMIT):
    return pltpu.CompilerParams(dimension_semantics=sem, vmem_limit_bytes=vmem)


def _rmsnorm_kernel(x_ref, g_ref, o_ref):
    x = x_ref[...].astype(F32)
    ms = jnp.mean(x * x, axis=-1, keepdims=True)
    o_ref[...] = (x * lax.rsqrt(ms + EPS) * g_ref[...]).astype(o_ref.dtype)


def _rmsnorm(x, gain, out_dtype, tm=256):
    t, d = x.shape
    tm = min(tm, t)
    return pl.pallas_call(
        _rmsnorm_kernel,
        grid=(t // tm,),
        in_specs=[pl.BlockSpec((tm, d), lambda i: (i, 0)), pl.BlockSpec((1, d), lambda i: (0, 0))],
        out_specs=pl.BlockSpec((tm, d), lambda i: (i, 0)),
        out_shape=jax.ShapeDtypeStruct((t, d), out_dtype),
        compiler_params=_params(("parallel",)),
        name="rmsnorm",
    )(x, gain.reshape(1, d).astype(F32))


def _matmul_kernel(*refs, n_pairs, has_add):
    o_ref = refs[-1]
    acc = None
    for p in range(n_pairs):
        part = jnp.dot(refs[p][...], refs[n_pairs + p][...], preferred_element_type=F32)
        acc = part if acc is None else acc + part
    if has_add:
        acc = acc + refs[2 * n_pairs][...].astype(F32)
    o_ref[...] = acc.astype(o_ref.dtype)


def _matmul(a_list, b_list, out_dtype, addend=None, tm=1024, tn=512, name="matmul"):
    m = a_list[0].shape[0]
    n = b_list[0].shape[1]
    tm = min(tm, m)
    tn = min(tn, n)
    while n % tn:
        tn -= LANES
    in_specs = [pl.BlockSpec((tm, a.shape[1]), lambda i, j: (i, 0)) for a in a_list]
    in_specs += [pl.BlockSpec((b.shape[0], tn), lambda i, j: (0, j)) for b in b_list]
    args = list(a_list) + list(b_list)
    if addend is not None:
        in_specs.append(pl.BlockSpec((tm, tn), lambda i, j: (i, j)))
        args.append(addend)
    return pl.pallas_call(
        functools.partial(_matmul_kernel, n_pairs=len(a_list), has_add=addend is not None),
        grid=(m // tm, n // tn),
        in_specs=in_specs,
        out_specs=pl.BlockSpec((tm, tn), lambda i, j: (i, j)),
        out_shape=jax.ShapeDtypeStruct((m, n), out_dtype),
        compiler_params=_params(("parallel", "parallel")),
        name=name,
    )(*args)


def _rope_tile(v, cos_t, sin_lo, sin_hi):
    return v * cos_t + pltpu.roll(v, LANES - 32, 1) * sin_lo + pltpu.roll(v, 32, 1) * sin_hi


def _mla_prep_kernel(cq_ref, ckv_ref, sm_ref, pos_ref, invf_ref, gqa_ref, gkva_ref, gq_ref, gk_ref,
                     wq_ref, wkv_ref, q_ref, k_ref, v_ref, *, n_heads, scale):
    tm = cq_ref.shape[0]
    lane = lax.broadcasted_iota(jnp.int32, (tm, LANES), 1)
    ang = pos_ref[...] * invf_ref[...]
    cos_t = jnp.where(lane < QK_ROPE_DIM, jnp.cos(ang), 0.0)
    sin_a = jnp.sin(ang)
    sin_lo = jnp.where(lane < 32, -sin_a, 0.0)
    sin_hi = jnp.where((lane >= 32) & (lane < QK_ROPE_DIM), sin_a, 0.0)

    def norm_rows(x, g):
        ms = jnp.mean(x * x, axis=-1, keepdims=True)
        return (x * lax.rsqrt(ms + EPS) * g).astype(BF16)

    cq = norm_rows(cq_ref[...].astype(F32), gqa_ref[...])
    ckv = norm_rows(ckv_ref[...].astype(F32), gkva_ref[...])
    qf = jnp.dot(cq, wq_ref[...], preferred_element_type=F32)
    kvf = jnp.dot(ckv, wkv_ref[...], preferred_element_type=F32)

    gq = gq_ref[...]
    gk = gk_ref[...]
    kr = jnp.where(lane < QK_ROPE_DIM, sm_ref[...], 0.0)
    kr_ss = jnp.sum(kr * kr, axis=-1, keepdims=True)
    kr_rot = _rope_tile(kr * gk[:, QK_NOPE_DIM:], cos_t, sin_lo, sin_hi)
    inv_dim = 1.0 / QK_HEAD_DIM
    for h in range(n_heads):
        qn = qf[:, QK_PAD_DIM * h:QK_PAD_DIM * h + QK_NOPE_DIM]
        qr = qf[:, QK_PAD_DIM * h + QK_NOPE_DIM:QK_PAD_DIM * (h + 1)]
        ss = jnp.sum(qn * qn, axis=-1, keepdims=True) + jnp.sum(qr * qr, axis=-1, keepdims=True)
        sc = lax.rsqrt(ss * inv_dim + EPS) * scale
        q_ref[:, QK_PAD_DIM * h:QK_PAD_DIM * h + QK_NOPE_DIM] = (qn * sc * gq[:, :QK_NOPE_DIM]).astype(BF16)
        q_ref[:, QK_PAD_DIM * h + QK_NOPE_DIM:QK_PAD_DIM * (h + 1)] = (
            _rope_tile(qr * gq[:, QK_NOPE_DIM:], cos_t, sin_lo, sin_hi) * sc).astype(BF16)
        kn = kvf[:, QK_NOPE_DIM * h:QK_NOPE_DIM * (h + 1)]
        ssk = jnp.sum(kn * kn, axis=-1, keepdims=True) + kr_ss
        sck = lax.rsqrt(ssk * inv_dim + EPS)
        k_ref[:, QK_PAD_DIM * h:QK_PAD_DIM * h + QK_NOPE_DIM] = (kn * sck * gk[:, :QK_NOPE_DIM]).astype(BF16)
        k_ref[:, QK_PAD_DIM * h + QK_NOPE_DIM:QK_PAD_DIM * (h + 1)] = (kr_rot * sck).astype(BF16)
    v_ref[...] = kvf[:, n_heads * QK_NOPE_DIM:].astype(BF16)


def _mla_prep(proj, small, pos_col, invf, gqa, gkva, gq, gk, wq, wkv, *, cq_blk, ckv_blk, n_heads, tm=256):
    t = proj.shape[0]
    tm = min(tm, t)
    q_lora, kv_lora = wq.shape[0], wkv.shape[0]
    const = lambda i: (0, 0)
    return pl.pallas_call(
        functools.partial(_mla_prep_kernel, n_heads=n_heads, scale=QK_HEAD_DIM ** -0.5),
        grid=(t // tm,),
        in_specs=[
            pl.BlockSpec((tm, q_lora), lambda i: (i, cq_blk)),
            pl.BlockSpec((tm, kv_lora), lambda i: (i, ckv_blk)),
            pl.BlockSpec((tm, LANES), lambda i: (i, 0)),
            pl.BlockSpec((tm, 1), lambda i: (i, 0)),
            pl.BlockSpec((1, LANES), const),
            pl.BlockSpec((1, q_lora), const),
            pl.BlockSpec((1, kv_lora), const),
            pl.BlockSpec((1, QK_PAD_DIM), const),
            pl.BlockSpec((1, QK_PAD_DIM), const),
            pl.BlockSpec(wq.shape, const),
            pl.BlockSpec(wkv.shape, const),
        ],
        out_specs=[
            pl.BlockSpec((tm, n_heads * QK_PAD_DIM), lambda i: (i, 0)),
            pl.BlockSpec((tm, n_heads * QK_PAD_DIM), lambda i: (i, 0)),
            pl.BlockSpec((tm, n_heads * V_HEAD_DIM), lambda i: (i, 0)),
        ],
        out_shape=[
            jax.ShapeDtypeStruct((t, n_heads * QK_PAD_DIM), BF16),
            jax.ShapeDtypeStruct((t, n_heads * QK_PAD_DIM), BF16),
            jax.ShapeDtypeStruct((t, n_heads * V_HEAD_DIM), BF16),
        ],
        compiler_params=_params(("parallel",)),
        name="mla_prep",
    )(proj, proj, small, pos_col, invf, gqa, gkva, gq, gk, wq, wkv)


def _attn_kernel(q_ref, k_ref, v_ref, g_ref, o_ref):
    s = lax.dot_general(q_ref[...], k_ref[...], (((1,), (1,)), ((), ())), preferred_element_type=F32)
    m = jnp.max(s, axis=-1, keepdims=True)
    p = jnp.exp(s - m)
    l = jnp.sum(p, axis=-1, keepdims=True)
    o = jnp.dot(p.astype(BF16), v_ref[...], preferred_element_type=F32) / l
    ms = jnp.mean(o * o, axis=-1, keepdims=True)
    o_ref[...] = (o * lax.rsqrt(ms + EPS) * g_ref[0]).astype(o_ref.dtype)


def _attention(q, k, v, gain, *, batch, seq, n_heads, tq=512):
    t = q.shape[0]
    tq = min(tq, seq)
    nq = seq // tq
    return pl.pallas_call(
        _attn_kernel,
        grid=(batch, n_heads, nq),
        in_specs=[
            pl.BlockSpec((tq, QK_PAD_DIM), lambda b, h, i: (b * nq + i, h)),
            pl.BlockSpec((seq, QK_PAD_DIM), lambda b, h, i: (b, h)),
            pl.BlockSpec((seq, V_HEAD_DIM), lambda b, h, i: (b, h)),
            pl.BlockSpec((1, 1, V_HEAD_DIM), lambda b, h, i: (h, 0, 0)),
        ],
        out_specs=pl.BlockSpec((tq, V_HEAD_DIM), lambda b, h, i: (b * nq + i, h)),
        out_shape=jax.ShapeDtypeStruct((t, n_heads * V_HEAD_DIM), BF16),
        compiler_params=_params(("parallel", "parallel", "parallel")),
        name="attention",
    )(q, k, v, gain)


def _softplus(x):
    return jnp.maximum(x, 0.0) + jnp.log1p(jnp.exp(-jnp.abs(x)))


def _silu(x):
    return x * jax.nn.sigmoid(x)


def _ssd_kernel(*refs, reverse, nc, n_heads, heads_per_group, finalize):
    if finalize:
        (xc_ref, xp_ref, xn_ref, sm_ref, smt_ref, cw_ref, cb_ref, bl_ref, al_ref, bc_ref, ac_ref,
         z_ref, yf_ref, dsk_ref, gn_ref, o_ref, xbc_s, y_s, state_s) = refs
    else:
        (xc_ref, xp_ref, xn_ref, sm_ref, smt_ref, cw_ref, cb_ref, bl_ref, al_ref, bc_ref, ac_ref,
         o_ref, xbc_s, y_s, state_s) = refs
    step = pl.program_id(1)
    c = (nc - 1 - step) if reverse else step
    width = n_heads * SSM_HEAD_DIM
    n_groups = n_heads // heads_per_group
    conv_dim = xc_ref.shape[1]

    @pl.when(step == 0)
    def _():
        state_s[...] = jnp.zeros_like(state_s)

    keep_prev = jnp.where(c == 0, 0.0, 1.0)
    keep_next = jnp.where(c == nc - 1, 0.0, 1.0)
    cblk = 512 if conv_dim % 512 == 0 else LANES
    for j in range(conv_dim // cblk):
        cs = slice(cblk * j, cblk * (j + 1))
        prev = xp_ref[:, cs].astype(F32)[HALO - 8:] * keep_prev
        nxt = xn_ref[:, cs].astype(F32)[:8] * keep_next
        ext = jnp.concatenate([prev, xc_ref[:, cs].astype(F32), nxt], axis=0)
        acc = cb_ref[:, cs] + cw_ref[0:1, cs] * ext[8 - D_CONV // 2:8 - D_CONV // 2 + CHUNK]
        for kk in range(1, D_CONV):
            off = 8 - D_CONV // 2 + kk
            acc = acc + cw_ref[kk:kk + 1, cs] * ext[off:off + CHUNK]
        xbc_s[:, cs] = _silu(acc)

    lane0 = QK_ROPE_DIM + (n_heads if reverse else 0)
    io_r = lax.broadcasted_iota(jnp.int32, (CHUNK, CHUNK), 0)
    io_c = lax.broadcasted_iota(jnp.int32, (CHUNK, CHUNK), 1)
    tri = (io_c >= io_r) if reverse else (io_c <= io_r)
    tri_t = (io_r >= io_c) if reverse else (io_r <= io_c)
    dtl = _softplus(sm_ref[...] + bl_ref[...])
    adt = dtl * (-jnp.exp(al_ref[...]))
    cum = jnp.dot(tri.astype(F32), adt, preferred_element_type=F32, precision=lax.Precision.HIGHEST)
    dtt = _softplus(smt_ref[...] + bc_ref[...])
    adtt = dtt * (-jnp.exp(ac_ref[...]))
    cum_t = jnp.dot(adtt, tri_t.astype(F32), preferred_element_type=F32, precision=lax.Precision.HIGHEST)
    end = 0 if reverse else CHUNK - 1
    total = cum[end:end + 1, :]
    to_end = jnp.exp(total - cum)
    from_start = jnp.exp(cum)
    chunk_decay = jnp.exp(total)

    lane = lax.broadcasted_iota(jnp.int32, (CHUNK, LANES), 1)
    lo_half = lane < SSM_HEAD_DIM

    def pair_cols(arr, h0):
        a = jnp.broadcast_to(arr[:, lane0 + h0:lane0 + h0 + 1], (arr.shape[0], LANES))
        b = jnp.broadcast_to(arr[:, lane0 + h0 + 1:lane0 + h0 + 2], (arr.shape[0], LANES))
        return jnp.where(lo_half[:arr.shape[0]], a, b)

    for g in range(n_groups):
        b_g = xbc_s[:, width + D_STATE * g:width + D_STATE * (g + 1)].astype(BF16)
        c_g = xbc_s[:, width + D_STATE * (n_groups + g):width + D_STATE * (n_groups + g + 1)].astype(BF16)
        cb = lax.dot_general(c_g, b_g, (((1,), (1,)), ((), ())), preferred_element_type=F32)
        for pp in range(heads_per_group // 2):
            h0 = g * heads_per_group + 2 * pp
            q = h0 // 2
            lg = []
            for e in range(2):
                col = cum[:, lane0 + h0 + e:lane0 + h0 + e + 1]
                row = cum_t[lane0 + h0 + e:lane0 + h0 + e + 1, :]
                seg = jnp.where(tri, col - row, -jnp.inf)
                lg.append((jnp.exp(seg) * cb).astype(BF16))
            lg = jnp.concatenate(lg, axis=1)
            xd = xbc_s[:, LANES * q:LANES * (q + 1)] * pair_cols(dtl, h0)
            xbd = jnp.concatenate([jnp.where(lo_half, xd, 0.0), jnp.where(lo_half, 0.0, xd)],
                                  axis=0).astype(BF16)
            y = jnp.dot(lg, xbd, preferred_element_type=F32)
            st = state_s[q]
            y = y + pair_cols(from_start, h0) * jnp.dot(c_g, st.astype(BF16), preferred_element_type=F32)
            xds = (xd * pair_cols(to_end, h0)).astype(BF16)
            state_s[q] = st * pair_cols(chunk_decay, h0) + lax.dot_general(
                b_g, xds, (((0,), (0,)), ((), ())), preferred_element_type=F32)
            y_s[:, LANES * q:LANES * (q + 1)] = y

    if not finalize:
        o_ref[...] = y_s[...]
    else:
        gsz = width // n_groups
        for g in range(n_groups):
            cs = slice(gsz * g, gsz * (g + 1))
            y = y_s[:, cs] + yf_ref[:, cs] + dsk_ref[:, cs] * xbc_s[:, cs]
            y = y * _silu(z_ref[:, cs].astype(F32))
            ms = jnp.mean(y * y, axis=-1, keepdims=True)
            o_ref[:, cs] = (y * lax.rsqrt(ms + EPS) * gn_ref[:, cs]).astype(o_ref.dtype)


def _ssd_pass(proj, small, small_t, cw, cb, bias_l, alog_l, bias_c, alog_c, *, batch, seq, n_heads,
              heads_per_group, conv_dim, reverse, final_args=None, z_blk=None):
    t = proj.shape[0]
    nc = seq // CHUNK
    width = n_heads * SSM_HEAD_DIM
    hb = CHUNK // HALO
    n_halo = t // HALO

    def chunk_of(b, s):
        return b * nc + ((nc - 1 - s) if reverse else s)

    const = lambda b, s: (0, 0)
    in_specs = [
        pl.BlockSpec((CHUNK, conv_dim), lambda b, s: (chunk_of(b, s), 0)),
        pl.BlockSpec((HALO, conv_dim), lambda b, s: (jnp.maximum(chunk_of(b, s) * hb - 1, 0), 0)),
        pl.BlockSpec((HALO, conv_dim), lambda b, s: (jnp.minimum((chunk_of(b, s) + 1) * hb, n_halo - 1), 0)),
        pl.BlockSpec((CHUNK, LANES), lambda b, s: (chunk_of(b, s), 0)),
        pl.BlockSpec((LANES, CHUNK), lambda b, s: (0, chunk_of(b, s))),
        pl.BlockSpec((D_CONV, conv_dim), const),
        pl.BlockSpec((1, conv_dim), const),
        pl.BlockSpec((1, LANES), const),
        pl.BlockSpec((1, LANES), const),
        pl.BlockSpec((LANES, 1), const),
        pl.BlockSpec((LANES, 1), const),
    ]
    args = [proj, proj, proj, small, small_t, cw, cb, bias_l, alog_l, bias_c, alog_c]
    finalize = final_args is not None
    if finalize:
        y_fwd, d_skip, gain = final_args
        in_specs += [
            pl.BlockSpec((CHUNK, width), lambda b, s: (chunk_of(b, s), z_blk)),
            pl.BlockSpec((CHUNK, width), lambda b, s: (chunk_of(b, s), 0)),
            pl.BlockSpec((1, width), const),
            pl.BlockSpec((1, width), const),
        ]
        args += [proj, y_fwd, d_skip, gain]
    return pl.pallas_call(
        functools.partial(_ssd_kernel, reverse=reverse, nc=nc, n_heads=n_heads,
                          heads_per_group=heads_per_group, finalize=finalize),
        grid=(batch, nc),
        in_specs=in_specs,
        out_specs=pl.BlockSpec((CHUNK, width), lambda b, s: (chunk_of(b, s), 0)),
        out_shape=jax.ShapeDtypeStruct((t, width), BF16 if finalize else F32),
        scratch_shapes=[
            pltpu.VMEM((CHUNK, conv_dim), F32),
            pltpu.VMEM((CHUNK, width), F32),
            pltpu.VMEM((n_heads // 2, D_STATE, LANES), F32),
        ],
        compiler_params=_params(("parallel", "arbitrary")),
        name="ssd_bwd" if reverse else "ssd_fwd",
    )(*args)


def _top16(s):
    rows, n = s.shape
    io = lax.broadcasted_iota(jnp.int32, s.shape, 0)
    io16 = lax.broadcasted_iota(jnp.int32, (PEER_TOPK, n), 0)
    rank = jnp.full(s.shape, float(PEER_TOPK), F32)
    vals = jnp.zeros((PEER_TOPK, n), F32)
    for i in range(PEER_TOPK):
        m = jnp.max(s, axis=0, keepdims=True)
        idx = jnp.min(jnp.where(s == m, io, rows), axis=0, keepdims=True)
        hit = io == idx
        rank = jnp.where(hit, float(i), rank)
        s = jnp.where(hit, -jnp.inf, s)
        vals = jnp.where(io16 == i, m, vals)
    return vals, rank


def _peer_topk_kernel(q_ref, keys_ref, a1_ref, jt_ref, a2_ref, r2_ref):
    half = q_ref.shape[1] // 2
    nt = (((1,), (1,)), ((), ()))
    s1 = lax.dot_general(keys_ref[0, 0], q_ref[:, :half], nt, preferred_element_type=F32)
    s2 = lax.dot_general(keys_ref[0, 1], q_ref[:, half:], nt, preferred_element_type=F32)
    v1, rank1 = _top16(s1)
    v2, rank2 = _top16(s2)
    cand = jnp.concatenate([v1[i:i + 1] + v2 for i in range(PEER_TOPK)], axis=0)
    _, crank = _top16(cand)
    sel = jnp.where(crank < PEER_TOPK, 1.0, 0.0)
    e1 = jnp.exp(v1 - v1[0:1])
    e2 = jnp.exp(v2 - v2[0:1])
    z = jnp.zeros_like(v1[0:1])
    jt = jnp.zeros_like(s1)
    for i in range(PEER_TOPK):
        sel_i = sel[PEER_TOPK * i:PEER_TOPK * (i + 1)]
        count = jnp.sum(sel_i, axis=0, keepdims=True)
        z = z + e1[i:i + 1] * jnp.sum(sel_i * e2, axis=0, keepdims=True)
        jt = jnp.where(rank1 == float(i), count, jt)
    a1_ref[0] = jnp.where(rank1 < PEER_TOPK, jnp.exp(s1 - v1[0:1]), 0.0)
    jt_ref[0] = jt
    a2_ref[0] = jnp.where(rank2 < PEER_TOPK, jnp.exp(s2 - v2[0:1]) / z, 0.0)
    r2_ref[0] = rank2


def _peer_topk(q, keys, *, tm=256):
    t = q.shape[0]
    n_heads, _, n_keys, half = keys.shape
    tm = min(tm, t)
    out_spec = pl.BlockSpec((1, n_keys, tm), lambda i, h: (h, 0, i))
    out_sds = jax.ShapeDtypeStruct((n_heads, n_keys, t), F32)
    return pl.pallas_call(
        _peer_topk_kernel,
        grid=(t // tm, n_heads),
        in_specs=[
            pl.BlockSpec((tm, 2 * half), lambda i, h: (i, h)),
            pl.BlockSpec((1, 2, n_keys, half), lambda i, h: (h, 0, 0, 0)),
        ],
        out_specs=[out_spec] * 4,
        out_shape=[out_sds] * 4,
        compiler_params=_params(("parallel", "parallel")),
        name="peer_topk",
    )(q, keys)


def _gelu_tanh(x):
    return 0.5 * x * (1.0 + jnp.tanh(0.7978845608028654 * (x + 0.044715 * (x * x * x))))


def _peer_expert_kernel(hn_ref, h_ref, u_ref, v_ref, a1_ref, jt_ref, a2_ref, r2_ref, o_ref, *, n_heads, rows):
    j = pl.program_id(1)

    @pl.when(j == 0)
    def _():
        o_ref[...] = h_ref[...]

    act = lax.dot_general(u_ref[...], hn_ref[...], (((1,), (1,)), ((), ())), preferred_element_type=F32)
    act = _gelu_tanh(act)
    gates = []
    for r in range(rows):
        e1 = j * rows + r
        g = None
        for h in range(n_heads):
            thr = jt_ref[h, pl.ds(e1, 1), :]
            a1 = a1_ref[h, pl.ds(e1, 1), :]
            term = jnp.where(r2_ref[h] < thr, a2_ref[h], 0.0) * a1
            g = term if g is None else g + term
        gates.append(g)
    gate = gates[0] if rows == 1 else jnp.concatenate(gates, axis=0)
    ga = (gate * act).astype(BF16)
    o_ref[...] += lax.dot_general(ga, v_ref[...], (((0,), (0,)), ((), ())), preferred_element_type=F32)


def _peer_experts(hn, h, eu, ev, a1, jt, a2, r2, *, tm=512, te=256):
    t, d = hn.shape
    n_heads, n_keys, _ = a1.shape
    n_exp = eu.shape[0]
    tm = min(tm, t)
    rows = te // n_keys
    once = pl.Buffered(1)
    tok = lambda i, j: (0, 0, i)
    return pl.pallas_call(
        functools.partial(_peer_expert_kernel, n_heads=n_heads, rows=rows),
        grid=(t // tm, n_exp // te),
        in_specs=[
            pl.BlockSpec((tm, d), lambda i, j: (i, 0), pipeline_mode=once),
            pl.BlockSpec((tm, d), lambda i, j: (i, 0), pipeline_mode=once),
            pl.BlockSpec((te, d), lambda i, j: (j, 0)),
            pl.BlockSpec((te, d), lambda i, j: (j, 0)),
            pl.BlockSpec((n_heads, n_keys, tm), tok, pipeline_mode=once),
            pl.BlockSpec((n_heads, n_keys, tm), tok, pipeline_mode=once),
            pl.BlockSpec((n_heads, n_keys, tm), tok, pipeline_mode=once),
            pl.BlockSpec((n_heads, n_keys, tm), tok, pipeline_mode=once),
        ],
        out_specs=pl.BlockSpec((tm, d), lambda i, j: (i, 0)),
        out_shape=jax.ShapeDtypeStruct((t, d), F32),
        compiler_params=_params(("parallel", "arbitrary")),
        name="peer_experts",
    )(hn, h, eu, ev, a1, jt, a2, r2)


def _layer(h, pos_col, p, *, batch, seq):
    t, d = h.shape
    n_mla = p["w_uq"].shape[1] // QK_HEAD_DIM
    q_lora, kv_lora = p["w_uq"].shape[0], p["w_ukv"].shape[0]
    n_ssm = p["a_log_fwd"].shape[0]
    ssm_width = n_ssm * SSM_HEAD_DIM
    conv_dim = p["conv_b"].shape[0]
    n_groups = (conv_dim - ssm_width) // (2 * D_STATE)
    hpg = n_ssm // n_groups

    sizes = (q_lora, kv_lora, QK_ROPE_DIM, ssm_width, conv_dim, n_ssm, n_ssm)
    offs = [0]
    for s_ in sizes:
        offs.append(offs[-1] + s_)
    w_in = p["w_in"]
    seg = lambda k: w_in[:, offs[k]:offs[k + 1]]
    w_main = jnp.concatenate([seg(4), seg(3), seg(0), seg(1)], axis=1).astype(BF16)
    small_pad = LANES - QK_ROPE_DIM - 2 * n_ssm
    w_small = jnp.concatenate([seg(2), seg(5), seg(6), jnp.zeros((d, small_pad), F32)], axis=1).astype(BF16)
    z_blk = conv_dim // ssm_width
    cq_blk = (conv_dim + ssm_width) // q_lora
    ckv_blk = (conv_dim + ssm_width + q_lora) // kv_lora

    wq = p["w_uq"].reshape(q_lora, n_mla, QK_HEAD_DIM)
    wq = jnp.pad(wq, ((0, 0), (0, 0), (0, QK_PAD_DIM - QK_HEAD_DIM))).reshape(q_lora, n_mla * QK_PAD_DIM).astype(BF16)
    wkv = p["w_ukv"].reshape(kv_lora, n_mla, 2, QK_NOPE_DIM).transpose(0, 2, 1, 3).reshape(kv_lora, -1).astype(BF16)
    pad_gain = lambda g: jnp.pad(g, (0, QK_PAD_DIM - QK_HEAD_DIM)).reshape(1, QK_PAD_DIM)
    half = QK_ROPE_DIM // 2
    invf = ROPE_BASE ** (-jnp.arange(half, dtype=F32) / half)
    invf = jnp.concatenate([invf, invf, jnp.zeros((LANES - QK_ROPE_DIM,), F32)]).reshape(1, LANES)

    def head_vec(fwd, bwd):
        v = jnp.concatenate([jnp.zeros((QK_ROPE_DIM,), F32), fwd, bwd, jnp.zeros((small_pad,), F32)])
        return v.reshape(1, LANES), v.reshape(LANES, 1)

    bias_l, bias_c = head_vec(p["dt_bias_fwd"], p["dt_bias_bwd"])
    alog_l, alog_c = head_vec(p["a_log_fwd"], p["a_log_bwd"])

    xn = _rmsnorm(h, p["norm_mix"], BF16)
    proj = _matmul([xn], [w_main], BF16, name="in_proj")
    small = _matmul([xn], [w_small], F32, name="in_proj_small")
    small_t = small.T

    q, k, v = _mla_prep(proj, small, pos_col, invf, p["q_a_norm"].reshape(1, -1), p["kv_a_norm"].reshape(1, -1),
                        pad_gain(p["q_norm"]), pad_gain(p["k_norm"]), wq, wkv,
                        cq_blk=cq_blk, ckv_blk=ckv_blk, n_heads=n_mla)
    attn = _attention(q, k, v, p["attn_out_norm"].reshape(n_mla, 1, V_HEAD_DIM), batch=batch, seq=seq, n_heads=n_mla)

    cw = p["conv_w"].reshape(D_CONV, conv_dim)
    cb = p["conv_b"].reshape(1, conv_dim)
    ssd_kw = dict(batch=batch, seq=seq, n_heads=n_ssm, heads_per_group=hpg, conv_dim=conv_dim)
    y_fwd = _ssd_pass(proj, small, small_t, cw, cb, bias_l, alog_l, bias_c, alog_c, reverse=False, **ssd_kw)
    d_skip = jnp.repeat(p["d_skip"], SSM_HEAD_DIM).reshape(1, ssm_width)
    ssm = _ssd_pass(proj, small, small_t, cw, cb, bias_l, alog_l, bias_c, alog_c, reverse=True,
                    final_args=(y_fwd, d_skip, p["ssm_out_norm"].reshape(1, ssm_width)), z_blk=z_blk, **ssd_kw)

    w_out = p["w_out"].astype(BF16)
    mla_width = n_mla * V_HEAD_DIM
    h = _matmul([attn, ssm], [w_out[:mla_width], w_out[mla_width:]], F32, addend=h, name="out_proj")

    hn = _rmsnorm(h, p["norm_ffn"], BF16)
    pq = _matmul([hn], [p["w_query"].astype(BF16)], BF16, name="peer_query")
    a1, jt, a2, r2 = _peer_topk(pq, p["sub_keys"].astype(BF16))
    return _peer_experts(hn, h, p["expert_u"].astype(BF16), p["expert_v"].astype(BF16), a1, jt, a2, r2)


_PARAM_NAMES = ("norm_mix", "w_in", "q_a_norm", "w_uq", "kv_a_norm", "w_ukv", "q_norm", "k_norm", "attn_out_norm",
                "conv_w", "conv_b", "a_log_fwd", "a_log_bwd", "dt_bias_fwd", "dt_bias_bwd", "d_skip",
                "ssm_out_norm", "w_out", "norm_ffn", "w_query", "sub_keys", "expert_u", "expert_v")


def kernel(x, positions, norm_mix, w_in, q_a_norm, w_uq, kv_a_norm, w_ukv, q_norm, k_norm, attn_out_norm, conv_w, conv_b, a_log_fwd, a_log_bwd, dt_bias_fwd, dt_bias_bwd, d_skip, ssm_out_norm, w_out, norm_ffn, w_query, sub_keys, expert_u, expert_v):
    weights = (norm_mix, w_in, q_a_norm, w_uq, kv_a_norm, w_ukv, q_norm, k_norm, attn_out_norm, conv_w, conv_b,
               a_log_fwd, a_log_bwd, dt_bias_fwd, dt_bias_bwd, d_skip, ssm_out_norm, w_out, norm_ffn, w_query,
               sub_keys, expert_u, expert_v)
    batch, seq, d = x.shape
    h = x.reshape(batch * seq, d)
    pos_col = positions.reshape(batch * seq, 1).astype(F32)
    for layer in range(norm_mix.shape[0]):
        p = {name: w[layer] for name, w in zip(_PARAM_NAMES, weights)}
        h = _layer(h, pos_col, p, batch=batch, seq=seq)
    return h.reshape(batch, seq, d).astype(x.dtype)
```

```python
import functools

import jax
import jax.numpy as jnp
from jax import lax
from jax.experimental import pallas as pl
from jax.experimental.pallas import tpu as pltpu

F32 = jnp.float32
BF16 = jnp.bfloat16

EPS = 1e-6
ROPE_BASE = 10000.0
QK_NOPE_DIM = 128
QK_ROPE_DIM = 64
QK_HEAD_DIM = QK_NOPE_DIM + QK_ROPE_DIM
QK_PAD_DIM = 256
V_HEAD_DIM = 128
SSM_HEAD_DIM = 64
D_STATE = 128
D_CONV = 5
CHUNK = 128
PEER_TOPK = 16
LANES = 128
HALO = 16
VMEM_LIMIT = 56 * 1024 * 1024
LOG2_E = 1.4426950408889634


def _params(sem, vmem=VMEM_LIMIT):
    return pltpu.CompilerParams(dimension_semantics=sem, vmem_limit_bytes=vmem)


def _rmsnorm_kernel(x_ref, g_ref, o_ref, *t_refs):
    x = x_ref[...].astype(F32)
    ms = jnp.mean(x * x, axis=-1, keepdims=True)
    y = x * lax.rsqrt(ms + EPS) * g_ref[...]
    o_ref[...] = y.astype(o_ref.dtype)
    for t_ref in t_refs:
        t_ref[...] = y.T.astype(t_ref.dtype)


def _rmsnorm(x, gain, out_dtype, tm=256, with_transpose=False):
    t, d = x.shape
    tm = min(tm, t)
    out_specs = [pl.BlockSpec((tm, d), lambda i: (i, 0))]
    out_shape = [jax.ShapeDtypeStruct((t, d), out_dtype)]
    if with_transpose:
        out_specs.append(pl.BlockSpec((d, tm), lambda i: (0, i)))
        out_shape.append(jax.ShapeDtypeStruct((d, t), out_dtype))
    out = pl.pallas_call(
        _rmsnorm_kernel,
        grid=(t // tm,),
        in_specs=[pl.BlockSpec((tm, d), lambda i: (i, 0)), pl.BlockSpec((1, d), lambda i: (0, 0))],
        out_specs=out_specs,
        out_shape=out_shape,
        compiler_params=_params(("parallel",)),
        name="rmsnorm_t" if with_transpose else "rmsnorm",
    )(x, gain.reshape(1, d).astype(F32))
    return out if with_transpose else out[0]


def _matmul_kernel(*refs, n_pairs, has_add):
    o_ref = refs[-1]
    acc = None
    for p in range(n_pairs):
        part = jnp.dot(refs[p][...], refs[n_pairs + p][...], preferred_element_type=F32)
        acc = part if acc is None else acc + part
    if has_add:
        acc = acc + refs[2 * n_pairs][...].astype(F32)
    o_ref[...] = acc.astype(o_ref.dtype)


def _matmul(a_list, b_list, out_dtype, addend=None, tm=1024, tn=512, name="matmul"):
    m = a_list[0].shape[0]
    n = b_list[0].shape[1]
    tm = min(tm, m)
    tn = min(tn, n)
    while n % tn:
        tn -= LANES
    in_specs = [pl.BlockSpec((tm, a.shape[1]), lambda i, j: (i, 0)) for a in a_list]
    in_specs += [pl.BlockSpec((b.shape[0], tn), lambda i, j: (0, j)) for b in b_list]
    args = list(a_list) + list(b_list)
    if addend is not None:
        in_specs.append(pl.BlockSpec((tm, tn), lambda i, j: (i, j)))
        args.append(addend)
    return pl.pallas_call(
        functools.partial(_matmul_kernel, n_pairs=len(a_list), has_add=addend is not None),
        grid=(m // tm, n // tn),
        in_specs=in_specs,
        out_specs=pl.BlockSpec((tm, tn), lambda i, j: (i, j)),
        out_shape=jax.ShapeDtypeStruct((m, n), out_dtype),
        compiler_params=_params(("parallel", "parallel")),
        name=name,
    )(*args)


def _rope_tile(v, cos_t, sin_lo, sin_hi):
    return v * cos_t + pltpu.roll(v, LANES - 32, 1) * sin_lo + pltpu.roll(v, 32, 1) * sin_hi


def _mla_prep_kernel(cq_ref, ckv_ref, sm_ref, pos_ref, invf_ref, gqa_ref, gkva_ref, gq_ref, gk_ref,
                     wq_ref, wkv_ref, q_ref, k_ref, v_ref, *, n_heads, scale):
    tm = cq_ref.shape[0]
    lane = lax.broadcasted_iota(jnp.int32, (tm, LANES), 1)
    ang = pos_ref[...] * invf_ref[...]
    cos_t = jnp.where(lane < QK_ROPE_DIM, jnp.cos(ang), 0.0)
    sin_a = jnp.sin(ang)
    sin_lo = jnp.where(lane < 32, -sin_a, 0.0)
    sin_hi = jnp.where((lane >= 32) & (lane < QK_ROPE_DIM), sin_a, 0.0)

    def norm_rows(x, g):
        ms = jnp.mean(x * x, axis=-1, keepdims=True)
        return (x * lax.rsqrt(ms + EPS) * g).astype(BF16)

    cq = norm_rows(cq_ref[...].astype(F32), gqa_ref[...])
    ckv = norm_rows(ckv_ref[...].astype(F32), gkva_ref[...])
    qf = jnp.dot(cq, wq_ref[...], preferred_element_type=F32)
    kvf = jnp.dot(ckv, wkv_ref[...], preferred_element_type=F32)

    gq = gq_ref[...]
    gk = gk_ref[...]
    kr = jnp.where(lane < QK_ROPE_DIM, sm_ref[...], 0.0)
    kr_ss = jnp.sum(kr * kr, axis=-1, keepdims=True)
    kr_rot = _rope_tile(kr * gk[:, QK_NOPE_DIM:], cos_t, sin_lo, sin_hi)
    inv_dim = 1.0 / QK_HEAD_DIM
    for h in range(n_heads):
        qn = qf[:, QK_PAD_DIM * h:QK_PAD_DIM * h + QK_NOPE_DIM]
        qr = qf[:, QK_PAD_DIM * h + QK_NOPE_DIM:QK_PAD_DIM * (h + 1)]
        ss = jnp.sum(qn * qn, axis=-1, keepdims=True) + jnp.sum(qr * qr, axis=-1, keepdims=True)
        sc = lax.rsqrt(ss * inv_dim + EPS) * scale
        q_ref[:, QK_PAD_DIM * h:QK_PAD_DIM * h + QK_NOPE_DIM] = (qn * sc * gq[:, :QK_NOPE_DIM]).astype(BF16)
        q_ref[:, QK_PAD_DIM * h + QK_NOPE_DIM:QK_PAD_DIM * (h + 1)] = (
            _rope_tile(qr * gq[:, QK_NOPE_DIM:], cos_t, sin_lo, sin_hi) * sc).astype(BF16)
        kn = kvf[:, QK_NOPE_DIM * h:QK_NOPE_DIM * (h + 1)]
        ssk = jnp.sum(kn * kn, axis=-1, keepdims=True) + kr_ss
        sck = lax.rsqrt(ssk * inv_dim + EPS)
        k_ref[:, QK_PAD_DIM * h:QK_PAD_DIM * h + QK_NOPE_DIM] = (kn * sck * gk[:, :QK_NOPE_DIM]).astype(BF16)
        k_ref[:, QK_PAD_DIM * h + QK_NOPE_DIM:QK_PAD_DIM * (h + 1)] = (kr_rot * sck).astype(BF16)
    v_ref[...] = kvf[:, n_heads * QK_NOPE_DIM:].astype(BF16)


def _mla_prep(proj, small, pos_col, invf, gqa, gkva, gq, gk, wq, wkv, *, cq_blk, ckv_blk, n_heads, tm=256):
    t = proj.shape[0]
    tm = min(tm, t)
    q_lora, kv_lora = wq.shape[0], wkv.shape[0]
    const = lambda i: (0, 0)
    return pl.pallas_call(
        functools.partial(_mla_prep_kernel, n_heads=n_heads, scale=QK_HEAD_DIM ** -0.5 * LOG2_E),
        grid=(t // tm,),
        in_specs=[
            pl.BlockSpec((tm, q_lora), lambda i: (i, cq_blk)),
            pl.BlockSpec((tm, kv_lora), lambda i: (i, ckv_blk)),
            pl.BlockSpec((tm, LANES), lambda i: (i, 0)),
            pl.BlockSpec((tm, 1), lambda i: (i, 0)),
            pl.BlockSpec((1, LANES), const),
            pl.BlockSpec((1, q_lora), const),
            pl.BlockSpec((1, kv_lora), const),
            pl.BlockSpec((1, QK_PAD_DIM), const),
            pl.BlockSpec((1, QK_PAD_DIM), const),
            pl.BlockSpec(wq.shape, const),
            pl.BlockSpec(wkv.shape, const),
        ],
        out_specs=[
            pl.BlockSpec((tm, n_heads * QK_PAD_DIM), lambda i: (i, 0)),
            pl.BlockSpec((tm, n_heads * QK_PAD_DIM), lambda i: (i, 0)),
            pl.BlockSpec((tm, n_heads * V_HEAD_DIM), lambda i: (i, 0)),
        ],
        out_shape=[
            jax.ShapeDtypeStruct((t, n_heads * QK_PAD_DIM), BF16),
            jax.ShapeDtypeStruct((t, n_heads * QK_PAD_DIM), BF16),
            jax.ShapeDtypeStruct((t, n_heads * V_HEAD_DIM), BF16),
        ],
        compiler_params=_params(("parallel",)),
        name="mla_prep",
    )(proj, proj, small, pos_col, invf, gqa, gkva, gq, gk, wq, wkv)


ATTN_SUB = 256


def _attn_kernel(q_ref, k_ref, v_ref, g_ref, o_ref, vaug_s):
    @pl.when(pl.program_id(2) == 0)
    def _():
        vaug_s[:, :V_HEAD_DIM] = v_ref[...]
        vaug_s[:, V_HEAD_DIM:] = jnp.ones((vaug_s.shape[0], V_HEAD_DIM), vaug_s.dtype)

    tq = q_ref.shape[0]
    sub = min(ATTN_SUB, tq)
    for r in range(tq // sub):
        rows = slice(sub * r, sub * (r + 1))
        s = lax.dot_general(q_ref[rows], k_ref[...], (((1,), (1,)), ((), ())), preferred_element_type=F32)
        m = jnp.max(s, axis=-1, keepdims=True)
        p = jnp.exp2(s - m).astype(BF16)
        oa = jnp.dot(p, vaug_s[...], preferred_element_type=F32)
        o = oa[:, :V_HEAD_DIM] / oa[:, V_HEAD_DIM:]
        ms = jnp.mean(o * o, axis=-1, keepdims=True)
        o_ref[rows] = (o * lax.rsqrt(ms + EPS) * g_ref[0]).astype(o_ref.dtype)


def _attention(q, k, v, gain, *, batch, seq, n_heads, tq=1024):
    t = q.shape[0]
    tq = min(tq, seq)
    nq = seq // tq
    return pl.pallas_call(
        _attn_kernel,
        grid=(batch, n_heads, nq),
        in_specs=[
            pl.BlockSpec((tq, QK_PAD_DIM), lambda b, h, i: (b * nq + i, h)),
            pl.BlockSpec((seq, QK_PAD_DIM), lambda b, h, i: (b, h)),
            pl.BlockSpec((seq, V_HEAD_DIM), lambda b, h, i: (b, h)),
            pl.BlockSpec((1, 1, V_HEAD_DIM), lambda b, h, i: (h, 0, 0)),
        ],
        out_specs=pl.BlockSpec((tq, V_HEAD_DIM), lambda b, h, i: (b * nq + i, h)),
        out_shape=jax.ShapeDtypeStruct((t, n_heads * V_HEAD_DIM), BF16),
        scratch_shapes=[pltpu.VMEM((seq, 2 * V_HEAD_DIM), BF16)],
        compiler_params=_params(("parallel", "parallel", "arbitrary")),
        name="attention",
    )(q, k, v, gain)


def _softplus(x):
    return jnp.maximum(x, 0.0) + jnp.log1p(jnp.exp(-jnp.abs(x)))


def _silu(x):
    return x * jax.nn.sigmoid(x)


def _ssd_kernel(*refs, reverse, nc, n_heads, heads_per_group, finalize):
    if finalize:
        (xc_ref, xp_ref, xn_ref, sm_ref, smt_ref, cw_ref, cb_ref, bl_ref, al_ref, bc_ref, ac_ref,
         z_ref, yf_ref, dsk_ref, gn_ref, o_ref, xbc_s, y_s, state_s) = refs
    else:
        (xc_ref, xp_ref, xn_ref, sm_ref, smt_ref, cw_ref, cb_ref, bl_ref, al_ref, bc_ref, ac_ref,
         o_ref, xbc_s, y_s, state_s) = refs
    step = pl.program_id(1)
    c = (nc - 1 - step) if reverse else step
    width = n_heads * SSM_HEAD_DIM
    n_groups = n_heads // heads_per_group
    conv_dim = xc_ref.shape[1]

    @pl.when(step == 0)
    def _():
        state_s[...] = jnp.zeros_like(state_s)

    keep_prev = jnp.where(c == 0, 0.0, 1.0)
    keep_next = jnp.where(c == nc - 1, 0.0, 1.0)
    cblk = 512 if conv_dim % 512 == 0 else LANES
    for j in range(conv_dim // cblk):
        cs = slice(cblk * j, cblk * (j + 1))
        prev = xp_ref[:, cs].astype(F32)[HALO - 8:] * keep_prev
        nxt = xn_ref[:, cs].astype(F32)[:8] * keep_next
        ext = jnp.concatenate([prev, xc_ref[:, cs].astype(F32), nxt], axis=0)
        acc = cb_ref[:, cs] + cw_ref[0:1, cs] * ext[8 - D_CONV // 2:8 - D_CONV // 2 + CHUNK]
        for kk in range(1, D_CONV):
            off = 8 - D_CONV // 2 + kk
            acc = acc + cw_ref[kk:kk + 1, cs] * ext[off:off + CHUNK]
        xbc_s[:, cs] = _silu(acc)

    lane0 = QK_ROPE_DIM + (n_heads if reverse else 0)
    io_r = lax.broadcasted_iota(jnp.int32, (CHUNK, CHUNK), 0)
    io_c = lax.broadcasted_iota(jnp.int32, (CHUNK, CHUNK), 1)
    tri = (io_c >= io_r) if reverse else (io_c <= io_r)
    tri_t = (io_r >= io_c) if reverse else (io_r <= io_c)
    dtl = _softplus(sm_ref[...] + bl_ref[...])
    adt = dtl * (-jnp.exp(al_ref[...]))
    cum = jnp.dot(tri.astype(F32), adt, preferred_element_type=F32, precision=lax.Precision.HIGHEST)
    dtt = _softplus(smt_ref[...] + bc_ref[...])
    adtt = dtt * (-jnp.exp(ac_ref[...]))
    cum_t = jnp.dot(adtt, tri_t.astype(F32), preferred_element_type=F32, precision=lax.Precision.HIGHEST)
    end = 0 if reverse else CHUNK - 1
    total = cum[end:end + 1, :]
    to_end = jnp.exp(total - cum)
    from_start = jnp.exp(cum)
    chunk_decay = jnp.exp(total)

    lane = lax.broadcasted_iota(jnp.int32, (CHUNK, LANES), 1)
    lo_half = lane < SSM_HEAD_DIM

    def pair_cols(arr, h0):
        a = jnp.broadcast_to(arr[:, lane0 + h0:lane0 + h0 + 1], (arr.shape[0], LANES))
        b = jnp.broadcast_to(arr[:, lane0 + h0 + 1:lane0 + h0 + 2], (arr.shape[0], LANES))
        return jnp.where(lo_half[:arr.shape[0]], a, b)

    for g in range(n_groups):
        b_g = xbc_s[:, width + D_STATE * g:width + D_STATE * (g + 1)].astype(BF16)
        c_g = xbc_s[:, width + D_STATE * (n_groups + g):width + D_STATE * (n_groups + g + 1)].astype(BF16)
        cb = lax.dot_general(c_g, b_g, (((1,), (1,)), ((), ())), preferred_element_type=F32)
        for pp in range(heads_per_group // 2):
            h0 = g * heads_per_group + 2 * pp
            q = h0 // 2
            lg = []
            for e in range(2):
                col = cum[:, lane0 + h0 + e:lane0 + h0 + e + 1]
                row = cum_t[lane0 + h0 + e:lane0 + h0 + e + 1, :]
                seg = jnp.where(tri, col - row, -jnp.inf)
                lg.append((jnp.exp(seg) * cb).astype(BF16))
            lg = jnp.concatenate(lg, axis=1)
            xd = xbc_s[:, LANES * q:LANES * (q + 1)] * pair_cols(dtl, h0)
            xbd = jnp.concatenate([jnp.where(lo_half, xd, 0.0), jnp.where(lo_half, 0.0, xd)],
                                  axis=0).astype(BF16)
            y = jnp.dot(lg, xbd, preferred_element_type=F32)
            st = state_s[q]
            y = y + pair_cols(from_start, h0) * jnp.dot(c_g, st.astype(BF16), preferred_element_type=F32)
            xds = (xd * pair_cols(to_end, h0)).astype(BF16)
            state_s[q] = st * pair_cols(chunk_decay, h0) + lax.dot_general(
                b_g, xds, (((0,), (0,)), ((), ())), preferred_element_type=F32)
            y_s[:, LANES * q:LANES * (q + 1)] = y

    if not finalize:
        o_ref[...] = y_s[...]
    else:
        gsz = width // n_groups
        for g in range(n_groups):
            cs = slice(gsz * g, gsz * (g + 1))
            y = y_s[:, cs] + yf_ref[:, cs] + dsk_ref[:, cs] * xbc_s[:, cs]
            y = y * _silu(z_ref[:, cs].astype(F32))
            ms = jnp.mean(y * y, axis=-1, keepdims=True)
            o_ref[:, cs] = (y * lax.rsqrt(ms + EPS) * gn_ref[:, cs]).astype(o_ref.dtype)


def _ssd_pass(proj, small, small_t, cw, cb, bias_l, alog_l, bias_c, alog_c, *, batch, seq, n_heads,
              heads_per_group, conv_dim, reverse, final_args=None, z_blk=None):
    t = proj.shape[0]
    nc = seq // CHUNK
    width = n_heads * SSM_HEAD_DIM
    hb = CHUNK // HALO
    n_halo = t // HALO

    def chunk_of(b, s):
        return b * nc + ((nc - 1 - s) if reverse else s)

    const = lambda b, s: (0, 0)
    in_specs = [
        pl.BlockSpec((CHUNK, conv_dim), lambda b, s: (chunk_of(b, s), 0)),
        pl.BlockSpec((HALO, conv_dim), lambda b, s: (jnp.maximum(chunk_of(b, s) * hb - 1, 0), 0)),
        pl.BlockSpec((HALO, conv_dim), lambda b, s: (jnp.minimum((chunk_of(b, s) + 1) * hb, n_halo - 1), 0)),
        pl.BlockSpec((CHUNK, LANES), lambda b, s: (chunk_of(b, s), 0)),
        pl.BlockSpec((LANES, CHUNK), lambda b, s: (0, chunk_of(b, s))),
        pl.BlockSpec((D_CONV, conv_dim), const),
        pl.BlockSpec((1, conv_dim), const),
        pl.BlockSpec((1, LANES), const),
        pl.BlockSpec((1, LANES), const),
        pl.BlockSpec((LANES, 1), const),
        pl.BlockSpec((LANES, 1), const),
    ]
    args = [proj, proj, proj, small, small_t, cw, cb, bias_l, alog_l, bias_c, alog_c]
    finalize = final_args is not None
    if finalize:
        y_fwd, d_skip, gain = final_args
        in_specs += [
            pl.BlockSpec((CHUNK, width), lambda b, s: (chunk_of(b, s), z_blk)),
            pl.BlockSpec((CHUNK, width), lambda b, s: (chunk_of(b, s), 0)),
            pl.BlockSpec((1, width), const),
            pl.BlockSpec((1, width), const),
        ]
        args += [proj, y_fwd, d_skip, gain]
    return pl.pallas_call(
        functools.partial(_ssd_kernel, reverse=reverse, nc=nc, n_heads=n_heads,
                          heads_per_group=heads_per_group, finalize=finalize),
        grid=(batch, nc),
        in_specs=in_specs,
        out_specs=pl.BlockSpec((CHUNK, width), lambda b, s: (chunk_of(b, s), 0)),
        out_shape=jax.ShapeDtypeStruct((t, width), BF16 if finalize else F32),
        scratch_shapes=[
            pltpu.VMEM((CHUNK, conv_dim), F32),
            pltpu.VMEM((CHUNK, width), F32),
            pltpu.VMEM((n_heads // 2, D_STATE, LANES), F32),
        ],
        compiler_params=_params(("parallel", "arbitrary")),
        name="ssd_bwd" if reverse else "ssd_fwd",
    )(*args)


def _top16(s):
    rows, n = s.shape
    io = lax.broadcasted_iota(jnp.int32, s.shape, 0).astype(F32)
    io16 = lax.broadcasted_iota(jnp.int32, (PEER_TOPK, n), 0)
    rank = jnp.full(s.shape, float(PEER_TOPK), F32)
    vals = jnp.zeros((PEER_TOPK, n), F32)
    for i in range(PEER_TOPK):
        m = jnp.max(s, axis=0, keepdims=True)
        idx = jnp.min(jnp.where(s == m, io, float(rows)), axis=0, keepdims=True)
        hit = io == idx
        rank = jnp.where(hit, float(i), rank)
        s = jnp.where(hit, -jnp.inf, s)
        vals = jnp.where(io16 == i, m, vals)
    return vals, rank


def _peer_topk_kernel(q_ref, keys_ref, a1_ref, jt_ref, a2_ref, r2_ref):
    half = q_ref.shape[1] // 2
    nt = (((1,), (1,)), ((), ()))
    s1 = lax.dot_general(keys_ref[0, 0], q_ref[:, :half], nt, preferred_element_type=F32)
    s2 = lax.dot_general(keys_ref[0, 1], q_ref[:, half:], nt, preferred_element_type=F32)
    v1, rank1 = _top16(s1)
    v2, rank2 = _top16(s2)
    n = v1.shape[1]
    sub8 = lax.broadcasted_iota(jnp.int32, (8, n), 0)
    blocks = [v1[0:1] + v2]
    for i in range(1, 8):
        blocks.append(jnp.where(sub8 < PEER_TOPK // (i + 1), v1[i:i + 1] + v2[0:8], -jnp.inf))
    blocks.append(v1[8:PEER_TOPK] + v2[0:1])
    _, crank = _top16(jnp.concatenate(blocks, axis=0))
    sel = jnp.where(crank < PEER_TOPK, 1.0, 0.0)
    e1 = jnp.exp(v1 - v1[0:1])
    e2 = jnp.exp(v2 - v2[0:1])
    tail = sel[PEER_TOPK + 56:]
    z = jnp.sum(sel[0:PEER_TOPK] * e2, axis=0, keepdims=True) + jnp.sum(tail * e1[8:], axis=0, keepdims=True)
    jt = jnp.where(rank1 == 0.0, jnp.sum(sel[0:PEER_TOPK], axis=0, keepdims=True), 0.0)
    for i in range(1, PEER_TOPK):
        if i < 8:
            sel_i = sel[PEER_TOPK + 8 * (i - 1):PEER_TOPK + 8 * i]
            count = jnp.sum(sel_i, axis=0, keepdims=True)
            z = z + e1[i:i + 1] * jnp.sum(sel_i * e2[0:8], axis=0, keepdims=True)
        else:
            count = tail[i - 8:i - 7]
        jt = jnp.where(rank1 == float(i), count, jt)
    a1_ref[0] = jnp.where(rank1 < PEER_TOPK, jnp.exp(s1 - v1[0:1]), 0.0)
    jt_ref[0] = jt
    a2_ref[0] = jnp.where(rank2 < PEER_TOPK, jnp.exp(s2 - v2[0:1]) / z, 0.0)
    r2_ref[0] = rank2


def _peer_topk(q, keys, *, tm=256):
    t = q.shape[0]
    n_heads, _, n_keys, half = keys.shape
    tm = min(tm, t)
    out_spec = pl.BlockSpec((1, n_keys, tm), lambda i, h: (h, 0, i))
    out_sds = jax.ShapeDtypeStruct((n_heads, n_keys, t), F32)
    return pl.pallas_call(
        _peer_topk_kernel,
        grid=(t // tm, n_heads),
        in_specs=[
            pl.BlockSpec((tm, 2 * half), lambda i, h: (i, h)),
            pl.BlockSpec((1, 2, n_keys, half), lambda i, h: (h, 0, 0, 0)),
        ],
        out_specs=[out_spec] * 4,
        out_shape=[out_sds] * 4,
        compiler_params=_params(("parallel", "parallel")),
        name="peer_topk",
    )(q, keys)


def _gelu_tanh(x):
    return 0.5 * x * (1.0 + jnp.tanh(0.7978845608028654 * (x + 0.044715 * (x * x * x))))


PEER_SUB = 256
PEER_KC = 256
PEER_NC = 512


def _peer_expert_kernel(hnt_ref, h_hbm, u_ref, v_ref, a1_ref, jt_ref, a2_ref, r2_ref, o_hbm, acc_ref,
                        gate_even, gate_odd, sem, *, n_heads, rows):
    i = pl.program_id(0)
    j = pl.program_id(1)
    nj = pl.num_programs(1)
    tm = acc_ref.shape[0]
    d = u_ref.shape[1]
    n_keys = r2_ref.shape[1]
    tok = pl.ds(pl.multiple_of(i * tm, tm), tm)

    def build_gate_tile(gate_ref, tile, r, ts):
        e1 = tile * rows + r
        g = None
        for h in range(n_heads):
            thr = jt_ref[h, pl.ds(e1, 1), :][:, ts]
            a1 = a1_ref[h, pl.ds(e1, 1), :][:, ts]
            term = jnp.where(r2_ref[h, :, ts] < thr, a2_ref[h, :, ts], 0.0) * a1
            g = term if g is None else g + term
        gate_ref[n_keys * r:n_keys * (r + 1), ts] = g

    @pl.when(j == 0)
    def _():
        cp = pltpu.make_async_copy(h_hbm.at[tok], acc_ref, sem)
        cp.start()
        for r in range(rows):
            build_gate_tile(gate_even, 0, r, slice(None))
        cp.wait()

    def step(gate_cur, gate_nxt):
        sub = min(PEER_SUB, tm)
        cols = [slice(sub * c, sub * (c + 1)) for c in range(tm // sub)]
        nxt = (j + 1) % nj
        gate_jobs = [(r, cb) for r in range(rows) for cb in range(tm // LANES)]
        per_piece = -(-len(gate_jobs) // (len(cols) * (d // PEER_KC)))

        def next_gate_tiles(count):
            for _ in range(count):
                if gate_jobs:
                    r, cb = gate_jobs.pop(0)
                    build_gate_tile(gate_nxt, nxt, r, slice(LANES * cb, LANES * (cb + 1)))

        acts = []
        for cs in cols:
            act = None
            for k0 in range(0, d, PEER_KC):
                part = jnp.dot(u_ref[:, k0:k0 + PEER_KC], hnt_ref[k0:k0 + PEER_KC, cs], preferred_element_type=F32)
                act = part if act is None else act + part
                next_gate_tiles(per_piece)
            acts.append(act)
        next_gate_tiles(len(gate_jobs))
        for cs, act in zip(cols, acts):
            ga = (gate_cur[:, cs] * _gelu_tanh(act)).astype(BF16)
            for n0 in range(0, d, PEER_NC):
                acc_ref[cs, n0:n0 + PEER_NC] += lax.dot_general(
                    ga, v_ref[:, n0:n0 + PEER_NC], (((0,), (0,)), ((), ())), preferred_element_type=F32)

    @pl.when(j % 2 == 0)
    def _():
        step(gate_even, gate_odd)

    @pl.when(j % 2 == 1)
    def _():
        step(gate_odd, gate_even)

    @pl.when(j == nj - 1)
    def _():
        cp = pltpu.make_async_copy(acc_ref, o_hbm.at[tok], sem)
        cp.start()
        cp.wait()


def _peer_experts(hnt, h, eu, ev, a1, jt, a2, r2, *, tm=512, te=512):
    d, t = hnt.shape
    n_heads, n_keys, _ = a1.shape
    n_exp = eu.shape[0]
    tm = min(tm, t)
    rows = te // n_keys
    once = pl.Buffered(1)
    tok = lambda i, j: (0, 0, i)
    return pl.pallas_call(
        functools.partial(_peer_expert_kernel, n_heads=n_heads, rows=rows),
        grid=(t // tm, n_exp // te),
        in_specs=[
            pl.BlockSpec((d, tm), lambda i, j: (0, i), pipeline_mode=once),
            pl.BlockSpec(memory_space=pl.ANY),
            pl.BlockSpec((te, d), lambda i, j: (j, 0)),
            pl.BlockSpec((te, d), lambda i, j: (j, 0)),
            pl.BlockSpec((n_heads, n_keys, tm), tok, pipeline_mode=once),
            pl.BlockSpec((n_heads, n_keys, tm), tok, pipeline_mode=once),
            pl.BlockSpec((n_heads, n_keys, tm), tok, pipeline_mode=once),
            pl.BlockSpec((n_heads, n_keys, tm), tok, pipeline_mode=once),
        ],
        out_specs=pl.BlockSpec(memory_space=pl.ANY),
        out_shape=jax.ShapeDtypeStruct((t, d), F32),
        scratch_shapes=[pltpu.VMEM((tm, d), F32), pltpu.VMEM((te, tm), F32), pltpu.VMEM((te, tm), F32),
                        pltpu.SemaphoreType.DMA(())],
        compiler_params=_params(("arbitrary", "arbitrary")),
        name="peer_experts",
    )(hnt, h, eu, ev, a1, jt, a2, r2)


def _layer(h, pos_col, p, *, batch, seq):
    t, d = h.shape
    n_mla = p["w_uq"].shape[1] // QK_HEAD_DIM
    q_lora, kv_lora = p["w_uq"].shape[0], p["w_ukv"].shape[0]
    n_ssm = p["a_log_fwd"].shape[0]
    ssm_width = n_ssm * SSM_HEAD_DIM
    conv_dim = p["conv_b"].shape[0]
    n_groups = (conv_dim - ssm_width) // (2 * D_STATE)
    hpg = n_ssm // n_groups

    sizes = (q_lora, kv_lora, QK_ROPE_DIM, ssm_width, conv_dim, n_ssm, n_ssm)
    offs = [0]
    for s_ in sizes:
        offs.append(offs[-1] + s_)
    w_in = p["w_in"]
    seg = lambda k: w_in[:, offs[k]:offs[k + 1]]
    w_main = jnp.concatenate([seg(4), seg(3), seg(0), seg(1)], axis=1).astype(BF16)
    small_pad = LANES - QK_ROPE_DIM - 2 * n_ssm
    w_small = jnp.concatenate([seg(2), seg(5), seg(6), jnp.zeros((d, small_pad), F32)], axis=1).astype(BF16)
    z_blk = conv_dim // ssm_width
    cq_blk = (conv_dim + ssm_width) // q_lora
    ckv_blk = (conv_dim + ssm_width + q_lora) // kv_lora

    wq = p["w_uq"].reshape(q_lora, n_mla, QK_HEAD_DIM)
    wq = jnp.pad(wq, ((0, 0), (0, 0), (0, QK_PAD_DIM - QK_HEAD_DIM))).reshape(q_lora, n_mla * QK_PAD_DIM).astype(BF16)
    wkv = p["w_ukv"].reshape(kv_lora, n_mla, 2, QK_NOPE_DIM).transpose(0, 2, 1, 3).reshape(kv_lora, -1).astype(BF16)
    pad_gain = lambda g: jnp.pad(g, (0, QK_PAD_DIM - QK_HEAD_DIM)).reshape(1, QK_PAD_DIM)
    half = QK_ROPE_DIM // 2
    invf = ROPE_BASE ** (-jnp.arange(half, dtype=F32) / half)
    invf = jnp.concatenate([invf, invf, jnp.zeros((LANES - QK_ROPE_DIM,), F32)]).reshape(1, LANES)

    def head_vec(fwd, bwd):
        v = jnp.concatenate([jnp.zeros((QK_ROPE_DIM,), F32), fwd, bwd, jnp.zeros((small_pad,), F32)])
        return v.reshape(1, LANES), v.reshape(LANES, 1)

    bias_l, bias_c = head_vec(p["dt_bias_fwd"], p["dt_bias_bwd"])
    alog_l, alog_c = head_vec(p["a_log_fwd"], p["a_log_bwd"])

    xn = _rmsnorm(h, p["norm_mix"], BF16)
    proj = _matmul([xn], [w_main], BF16, name="in_proj")
    small = _matmul([xn], [w_small], F32, name="in_proj_small")
    small_t = small.T

    q, k, v = _mla_prep(proj, small, pos_col, invf, p["q_a_norm"].reshape(1, -1), p["kv_a_norm"].reshape(1, -1),
                        pad_gain(p["q_norm"]), pad_gain(p["k_norm"]), wq, wkv,
                        cq_blk=cq_blk, ckv_blk=ckv_blk, n_heads=n_mla)
    attn = _attention(q, k, v, p["attn_out_norm"].reshape(n_mla, 1, V_HEAD_DIM), batch=batch, seq=seq, n_heads=n_mla)

    cw = p["conv_w"].reshape(D_CONV, conv_dim)
    cb = p["conv_b"].reshape(1, conv_dim)
    ssd_kw = dict(batch=batch, seq=seq, n_heads=n_ssm, heads_per_group=hpg, conv_dim=conv_dim)
    y_fwd = _ssd_pass(proj, small, small_t, cw, cb, bias_l, alog_l, bias_c, alog_c, reverse=False, **ssd_kw)
    d_skip = jnp.repeat(p["d_skip"], SSM_HEAD_DIM).reshape(1, ssm_width)
    ssm = _ssd_pass(proj, small, small_t, cw, cb, bias_l, alog_l, bias_c, alog_c, reverse=True,
                    final_args=(y_fwd, d_skip, p["ssm_out_norm"].reshape(1, ssm_width)), z_blk=z_blk, **ssd_kw)

    w_out = p["w_out"].astype(BF16)
    mla_width = n_mla * V_HEAD_DIM
    h = _matmul([attn, ssm], [w_out[:mla_width], w_out[mla_width:]], F32, addend=h, name="out_proj")

    hn, hnt = _rmsnorm(h, p["norm_ffn"], BF16, with_transpose=True)
    pq = _matmul([hn], [p["w_query"].astype(BF16)], BF16, name="peer_query")
    a1, jt, a2, r2 = _peer_topk(pq, p["sub_keys"].astype(BF16))
    return _peer_experts(hnt, h, p["expert_u"].astype(BF16), p["expert_v"].astype(BF16), a1, jt, a2, r2)


_PARAM_NAMES = ("norm_mix", "w_in", "q_a_norm", "w_uq", "kv_a_norm", "w_ukv", "q_norm", "k_norm", "attn_out_norm",
                "conv_w", "conv_b", "a_log_fwd", "a_log_bwd", "dt_bias_fwd", "dt_bias_bwd", "d_skip",
                "ssm_out_norm", "w_out", "norm_ffn", "w_query", "sub_keys", "expert_u", "expert_v")


def kernel(x, positions, norm_mix, w_in, q_a_norm, w_uq, kv_a_norm, w_ukv, q_norm, k_norm, attn_out_norm, conv_w, conv_b, a_log_fwd, a_log_bwd, dt_bias_fwd, dt_bias_bwd, d_skip, ssm_out_norm, w_out, norm_ffn, w_query, sub_keys, expert_u, expert_v):
    weights = (norm_mix, w_in, q_a_norm, w_uq, kv_a_norm, w_ukv, q_norm, k_norm, attn_out_norm, conv_w, conv_b,
               a_log_fwd, a_log_bwd, dt_bias_fwd, dt_bias_bwd, d_skip, ssm_out_norm, w_out, norm_ffn, w_query,
               sub_keys, expert_u, expert_v)
    batch, seq, d = x.shape
    h = x.reshape(batch * seq, d)
    pos_col = positions.reshape(batch * seq, 1).astype(F32)
    for layer in range(norm_mix.shape[0]):
        p = {name: w[layer] for name, w in zip(_PARAM_NAMES, weights)}
        h = _layer(h, pos_col, p, batch=batch, seq=seq)
    return h.reshape(batch, seq, d).astype(x.dtype)
```

```python
import functools

import jax
import jax.numpy as jnp
from jax import lax
from jax.experimental import pallas as pl
from jax.experimental.pallas import tpu as pltpu

F32 = jnp.float32
BF16 = jnp.bfloat16

EPS = 1e-6
ROPE_BASE = 10000.0
QK_NOPE_DIM = 128
QK_ROPE_DIM = 64
QK_HEAD_DIM = QK_NOPE_DIM + QK_ROPE_DIM
QK_PAD_DIM = 256
V_HEAD_DIM = 128
SSM_HEAD_DIM = 64
D_STATE = 128
D_CONV = 5
CHUNK = 128
PEER_TOPK = 16
LANES = 128
HALO = 16
VMEM_LIMIT = 56 * 1024 * 1024
LOG2_E = 1.4426950408889634


def _params(sem, vmem=VMEM_LIMIT):
    return pltpu.CompilerParams(dimension_semantics=sem, vmem_limit_bytes=vmem)


def _rmsnorm_kernel(x_ref, g_ref, o_ref, *t_refs):
    x = x_ref[...].astype(F32)
    ms = jnp.mean(x * x, axis=-1, keepdims=True)
    y = x * lax.rsqrt(ms + EPS) * g_ref[...]
    o_ref[...] = y.astype(o_ref.dtype)
    for t_ref in t_refs:
        t_ref[...] = y.T.astype(t_ref.dtype)


def _rmsnorm(x, gain, out_dtype, tm=256, with_transpose=False):
    t, d = x.shape
    tm = min(tm, t)
    out_specs = [pl.BlockSpec((tm, d), lambda i: (i, 0))]
    out_shape = [jax.ShapeDtypeStruct((t, d), out_dtype)]
    if with_transpose:
        out_specs.append(pl.BlockSpec((d, tm), lambda i: (0, i)))
        out_shape.append(jax.ShapeDtypeStruct((d, t), out_dtype))
    out = pl.pallas_call(
        _rmsnorm_kernel,
        grid=(t // tm,),
        in_specs=[pl.BlockSpec((tm, d), lambda i: (i, 0)), pl.BlockSpec((1, d), lambda i: (0, 0))],
        out_specs=out_specs,
        out_shape=out_shape,
        compiler_params=_params(("parallel",)),
        name="rmsnorm_t" if with_transpose else "rmsnorm",
    )(x, gain.reshape(1, d).astype(F32))
    return out if with_transpose else out[0]


def _matmul_kernel(*refs, n_pairs, has_add):
    o_ref = refs[-1]
    acc = None
    for p in range(n_pairs):
        part = jnp.dot(refs[p][...], refs[n_pairs + p][...], preferred_element_type=F32)
        acc = part if acc is None else acc + part
    if has_add:
        acc = acc + refs[2 * n_pairs][...].astype(F32)
    o_ref[...] = acc.astype(o_ref.dtype)


def _matmul(a_list, b_list, out_dtype, addend=None, tm=1024, tn=512, name="matmul"):
    m = a_list[0].shape[0]
    n = b_list[0].shape[1]
    tm = min(tm, m)
    tn = min(tn, n)
    while n % tn:
        tn -= LANES
    in_specs = [pl.BlockSpec((tm, a.shape[1]), lambda i, j: (i, 0)) for a in a_list]
    in_specs += [pl.BlockSpec((b.shape[0], tn), lambda i, j: (0, j)) for b in b_list]
    args = list(a_list) + list(b_list)
    if addend is not None:
        in_specs.append(pl.BlockSpec((tm, tn), lambda i, j: (i, j)))
        args.append(addend)
    return pl.pallas_call(
        functools.partial(_matmul_kernel, n_pairs=len(a_list), has_add=addend is not None),
        grid=(m // tm, n // tn),
        in_specs=in_specs,
        out_specs=pl.BlockSpec((tm, tn), lambda i, j: (i, j)),
        out_shape=jax.ShapeDtypeStruct((m, n), out_dtype),
        compiler_params=_params(("parallel", "parallel")),
        name=name,
    )(*args)


def _rope_tile(v, cos_t, sin_lo, sin_hi):
    return v * cos_t + pltpu.roll(v, LANES - 32, 1) * sin_lo + pltpu.roll(v, 32, 1) * sin_hi


def _mla_prep_kernel(cq_ref, ckv_ref, sm_ref, pos_ref, invf_ref, gqa_ref, gkva_ref, gq_ref, gk_ref,
                     wq_ref, wkv_ref, q_ref, k_ref, v_ref, *, n_heads, scale):
    tm = cq_ref.shape[0]
    lane = lax.broadcasted_iota(jnp.int32, (tm, LANES), 1)
    ang = pos_ref[...] * invf_ref[...]
    cos_t = jnp.where(lane < QK_ROPE_DIM, jnp.cos(ang), 0.0)
    sin_a = jnp.sin(ang)
    sin_lo = jnp.where(lane < 32, -sin_a, 0.0)
    sin_hi = jnp.where((lane >= 32) & (lane < QK_ROPE_DIM), sin_a, 0.0)

    def norm_rows(x, g):
        ms = jnp.mean(x * x, axis=-1, keepdims=True)
        return (x * lax.rsqrt(ms + EPS) * g).astype(BF16)

    cq = norm_rows(cq_ref[...].astype(F32), gqa_ref[...])
    ckv = norm_rows(ckv_ref[...].astype(F32), gkva_ref[...])
    qf = jnp.dot(cq, wq_ref[...], preferred_element_type=F32)
    kvf = jnp.dot(ckv, wkv_ref[...], preferred_element_type=F32)

    gq = gq_ref[...]
    gk = gk_ref[...]
    kr = jnp.where(lane < QK_ROPE_DIM, sm_ref[...], 0.0)
    kr_ss = jnp.sum(kr * kr, axis=-1, keepdims=True)
    kr_rot = _rope_tile(kr * gk[:, QK_NOPE_DIM:], cos_t, sin_lo, sin_hi)
    inv_dim = 1.0 / QK_HEAD_DIM
    for h in range(n_heads):
        qn = qf[:, QK_PAD_DIM * h:QK_PAD_DIM * h + QK_NOPE_DIM]
        qr = qf[:, QK_PAD_DIM * h + QK_NOPE_DIM:QK_PAD_DIM * (h + 1)]
        ss = jnp.sum(qn * qn, axis=-1, keepdims=True) + jnp.sum(qr * qr, axis=-1, keepdims=True)
        sc = lax.rsqrt(ss * inv_dim + EPS) * scale
        q_ref[:, QK_PAD_DIM * h:QK_PAD_DIM * h + QK_NOPE_DIM] = (qn * sc * gq[:, :QK_NOPE_DIM]).astype(BF16)
        q_ref[:, QK_PAD_DIM * h + QK_NOPE_DIM:QK_PAD_DIM * (h + 1)] = (
            _rope_tile(qr * gq[:, QK_NOPE_DIM:], cos_t, sin_lo, sin_hi) * sc).astype(BF16)
        kn = kvf[:, QK_NOPE_DIM * h:QK_NOPE_DIM * (h + 1)]
        ssk = jnp.sum(kn * kn, axis=-1, keepdims=True) + kr_ss
        sck = lax.rsqrt(ssk * inv_dim + EPS)
        k_ref[:, QK_PAD_DIM * h:QK_PAD_DIM * h + QK_NOPE_DIM] = (kn * sck * gk[:, :QK_NOPE_DIM]).astype(BF16)
        k_ref[:, QK_PAD_DIM * h + QK_NOPE_DIM:QK_PAD_DIM * (h + 1)] = (kr_rot * sck).astype(BF16)
    v_ref[...] = kvf[:, n_heads * QK_NOPE_DIM:].astype(BF16)


def _mla_prep(proj, small, pos_col, invf, gqa, gkva, gq, gk, wq, wkv, *, cq_blk, ckv_blk, n_heads, tm=256):
    t = proj.shape[0]
    tm = min(tm, t)
    q_lora, kv_lora = wq.shape[0], wkv.shape[0]
    const = lambda i: (0, 0)
    return pl.pallas_call(
        functools.partial(_mla_prep_kernel, n_heads=n_heads, scale=QK_HEAD_DIM ** -0.5 * LOG2_E),
        grid=(t // tm,),
        in_specs=[
            pl.BlockSpec((tm, q_lora), lambda i: (i, cq_blk)),
            pl.BlockSpec((tm, kv_lora), lambda i: (i, ckv_blk)),
            pl.BlockSpec((tm, LANES), lambda i: (i, 0)),
            pl.BlockSpec((tm, 1), lambda i: (i, 0)),
            pl.BlockSpec((1, LANES), const),
            pl.BlockSpec((1, q_lora), const),
            pl.BlockSpec((1, kv_lora), const),
            pl.BlockSpec((1, QK_PAD_DIM), const),
            pl.BlockSpec((1, QK_PAD_DIM), const),
            pl.BlockSpec(wq.shape, const),
            pl.BlockSpec(wkv.shape, const),
        ],
        out_specs=[
            pl.BlockSpec((tm, n_heads * QK_PAD_DIM), lambda i: (i, 0)),
            pl.BlockSpec((tm, n_heads * QK_PAD_DIM), lambda i: (i, 0)),
            pl.BlockSpec((tm, n_heads * V_HEAD_DIM), lambda i: (i, 0)),
        ],
        out_shape=[
            jax.ShapeDtypeStruct((t, n_heads * QK_PAD_DIM), BF16),
            jax.ShapeDtypeStruct((t, n_heads * QK_PAD_DIM), BF16),
            jax.ShapeDtypeStruct((t, n_heads * V_HEAD_DIM), BF16),
        ],
        compiler_params=_params(("parallel",)),
        name="mla_prep",
    )(proj, proj, small, pos_col, invf, gqa, gkva, gq, gk, wq, wkv)


ATTN_SUB = 256


def _attn_kernel(q_ref, k_ref, v_ref, g_ref, o_ref, vaug_s):
    @pl.when(pl.program_id(2) == 0)
    def _():
        vaug_s[:, :V_HEAD_DIM] = v_ref[...]
        vaug_s[:, V_HEAD_DIM:] = jnp.ones((vaug_s.shape[0], V_HEAD_DIM), vaug_s.dtype)

    tq = q_ref.shape[0]
    sub = min(ATTN_SUB, tq)
    for r in range(tq // sub):
        rows = slice(sub * r, sub * (r + 1))
        s = lax.dot_general(q_ref[rows], k_ref[...], (((1,), (1,)), ((), ())), preferred_element_type=F32)
        m = jnp.max(s, axis=-1, keepdims=True)
        p = jnp.exp2(s - m).astype(BF16)
        oa = jnp.dot(p, vaug_s[...], preferred_element_type=F32)
        o = oa[:, :V_HEAD_DIM] / oa[:, V_HEAD_DIM:]
        ms = jnp.mean(o * o, axis=-1, keepdims=True)
        o_ref[rows] = (o * lax.rsqrt(ms + EPS) * g_ref[0]).astype(o_ref.dtype)


def _attention(q, k, v, gain, *, batch, seq, n_heads, tq=2048):
    t = q.shape[0]
    tq = min(tq, seq)
    nq = seq // tq
    return pl.pallas_call(
        _attn_kernel,
        grid=(batch, n_heads, nq),
        in_specs=[
            pl.BlockSpec((tq, QK_PAD_DIM), lambda b, h, i: (b * nq + i, h)),
            pl.BlockSpec((seq, QK_PAD_DIM), lambda b, h, i: (b, h)),
            pl.BlockSpec((seq, V_HEAD_DIM), lambda b, h, i: (b, h)),
            pl.BlockSpec((1, 1, V_HEAD_DIM), lambda b, h, i: (h, 0, 0)),
        ],
        out_specs=pl.BlockSpec((tq, V_HEAD_DIM), lambda b, h, i: (b * nq + i, h)),
        out_shape=jax.ShapeDtypeStruct((t, n_heads * V_HEAD_DIM), BF16),
        scratch_shapes=[pltpu.VMEM((seq, 2 * V_HEAD_DIM), BF16)],
        compiler_params=_params(("parallel", "parallel", "arbitrary")),
        name="attention",
    )(q, k, v, gain)


def _softplus(x):
    return jnp.maximum(x, 0.0) + jnp.log1p(jnp.exp(-jnp.abs(x)))


def _silu(x):
    return x * jax.nn.sigmoid(x)


def _ssd_kernel(*refs, reverse, nc, n_heads, heads_per_group, finalize):
    if finalize:
        (xbc_ref, sm_ref, smt_ref, bl_ref, al_ref, bc_ref, ac_ref,
         z_ref, yf_ref, dsk_ref, gn_ref, o_ref, y_s, state_s) = refs
    else:
        (xc_ref, xp_ref, xn_ref, sm_ref, smt_ref, cw_ref, cb_ref, bl_ref, al_ref, bc_ref, ac_ref,
         o_ref, xbc_ref, ext_s, y_s, state_s) = refs
    step = pl.program_id(1)
    c = (nc - 1 - step) if reverse else step
    width = n_heads * SSM_HEAD_DIM
    n_groups = n_heads // heads_per_group

    @pl.when(step == 0)
    def _():
        state_s[...] = jnp.zeros_like(state_s)

    if not finalize:
        conv_dim = xc_ref.shape[1]
        keep_prev = jnp.where(c == 0, 0.0, 1.0)
        keep_next = jnp.where(c == nc - 1, 0.0, 1.0)
        ext_s[0:8, :] = xp_ref[...].astype(F32)[HALO - 8:] * keep_prev
        ext_s[8:8 + CHUNK, :] = xc_ref[...].astype(F32)
        ext_s[8 + CHUNK:, :] = xn_ref[...].astype(F32)[:8] * keep_next
        cblk = 512 if conv_dim % 512 == 0 else LANES
        for j in range(conv_dim // cblk):
            cs = slice(cblk * j, cblk * (j + 1))
            acc = cb_ref[:, cs]
            for kk in range(D_CONV):
                off = 8 - D_CONV // 2 + kk
                acc = acc + cw_ref[kk:kk + 1, cs] * ext_s[off:off + CHUNK, cs]
            xbc_ref[:, cs] = _silu(acc).astype(xbc_ref.dtype)

    lane0 = QK_ROPE_DIM + (n_heads if reverse else 0)
    io_r = lax.broadcasted_iota(jnp.int32, (CHUNK, CHUNK), 0)
    io_c = lax.broadcasted_iota(jnp.int32, (CHUNK, CHUNK), 1)
    tri = (io_c >= io_r) if reverse else (io_c <= io_r)
    tri_t = (io_r >= io_c) if reverse else (io_r <= io_c)
    dtl = _softplus(sm_ref[...] + bl_ref[...])
    adt = dtl * (-jnp.exp(al_ref[...]))
    cum = jnp.dot(tri.astype(F32), adt, preferred_element_type=F32, precision=lax.Precision.HIGHEST)
    dtt = _softplus(smt_ref[...] + bc_ref[...])
    adtt = dtt * (-jnp.exp(ac_ref[...]))
    cum_t = jnp.dot(adtt, tri_t.astype(F32), preferred_element_type=F32, precision=lax.Precision.HIGHEST)
    end = 0 if reverse else CHUNK - 1
    total = cum[end:end + 1, :]
    to_end = jnp.exp(total - cum)
    from_start = jnp.exp(cum)
    chunk_decay = jnp.exp(total)

    lane = lax.broadcasted_iota(jnp.int32, (CHUNK, LANES), 1)
    lo_half = lane < SSM_HEAD_DIM

    def pair_cols(arr, h0):
        a = jnp.broadcast_to(arr[:, lane0 + h0:lane0 + h0 + 1], (arr.shape[0], LANES))
        b = jnp.broadcast_to(arr[:, lane0 + h0 + 1:lane0 + h0 + 2], (arr.shape[0], LANES))
        return jnp.where(lo_half[:arr.shape[0]], a, b)

    for g in range(n_groups):
        b_g = xbc_ref[:, width + D_STATE * g:width + D_STATE * (g + 1)]
        c_g = xbc_ref[:, width + D_STATE * (n_groups + g):width + D_STATE * (n_groups + g + 1)]
        cb = lax.dot_general(c_g, b_g, (((1,), (1,)), ((), ())), preferred_element_type=F32)
        for pp in range(heads_per_group // 2):
            h0 = g * heads_per_group + 2 * pp
            q = h0 // 2
            lg = []
            for e in range(2):
                col = cum[:, lane0 + h0 + e:lane0 + h0 + e + 1]
                row = cum_t[lane0 + h0 + e:lane0 + h0 + e + 1, :]
                seg = jnp.where(tri, col - row, -jnp.inf)
                lg.append((jnp.exp(seg) * cb).astype(BF16))
            lg = jnp.concatenate(lg, axis=1)
            xd = xbc_ref[:, LANES * q:LANES * (q + 1)].astype(F32) * pair_cols(dtl, h0)
            xbd = jnp.concatenate([jnp.where(lo_half, xd, 0.0), jnp.where(lo_half, 0.0, xd)],
                                  axis=0).astype(BF16)
            y = jnp.dot(lg, xbd, preferred_element_type=F32)
            st = state_s[q]
            y = y + pair_cols(from_start, h0) * jnp.dot(c_g, st.astype(BF16), preferred_element_type=F32)
            xds = (xd * pair_cols(to_end, h0)).astype(BF16)
            state_s[q] = st * pair_cols(chunk_decay, h0) + lax.dot_general(
                b_g, xds, (((0,), (0,)), ((), ())), preferred_element_type=F32)
            y_s[:, LANES * q:LANES * (q + 1)] = y

    if not finalize:
        o_ref[...] = y_s[...]
    else:
        gsz = width // n_groups
        for g in range(n_groups):
            cs = slice(gsz * g, gsz * (g + 1))
            y = y_s[:, cs] + yf_ref[:, cs] + dsk_ref[:, cs] * xbc_ref[:, cs].astype(F32)
            y = y * _silu(z_ref[:, cs].astype(F32))
            ms = jnp.mean(y * y, axis=-1, keepdims=True)
            o_ref[:, cs] = (y * lax.rsqrt(ms + EPS) * gn_ref[:, cs]).astype(o_ref.dtype)


def _ssd_pass(proj, small, small_t, cw, cb, bias_l, alog_l, bias_c, alog_c, *, batch, seq, n_heads,
              heads_per_group, conv_dim, reverse, final_args=None, z_blk=None):
    t = proj.shape[0]
    nc = seq // CHUNK
    width = n_heads * SSM_HEAD_DIM
    hb = CHUNK // HALO
    n_halo = t // HALO

    def chunk_of(b, s):
        return b * nc + ((nc - 1 - s) if reverse else s)

    const = lambda b, s: (0, 0)
    chunk_rows = lambda b, s: (chunk_of(b, s), 0)
    scan_specs = [
        pl.BlockSpec((CHUNK, LANES), chunk_rows),
        pl.BlockSpec((LANES, CHUNK), lambda b, s: (0, chunk_of(b, s))),
    ]
    head_specs = [
        pl.BlockSpec((1, LANES), const),
        pl.BlockSpec((1, LANES), const),
        pl.BlockSpec((LANES, 1), const),
        pl.BlockSpec((LANES, 1), const),
    ]
    y_spec = pl.BlockSpec((CHUNK, width), chunk_rows)
    scratch = [pltpu.VMEM((CHUNK, width), F32), pltpu.VMEM((n_heads // 2, D_STATE, LANES), F32)]
    finalize = final_args is not None
    if finalize:
        xbc, y_fwd, d_skip, gain = final_args
        in_specs = [pl.BlockSpec((CHUNK, conv_dim), chunk_rows)] + scan_specs + head_specs + [
            pl.BlockSpec((CHUNK, width), lambda b, s: (chunk_of(b, s), z_blk)),
            y_spec,
            pl.BlockSpec((1, width), const),
            pl.BlockSpec((1, width), const),
        ]
        args = [xbc, small, small_t, bias_l, alog_l, bias_c, alog_c, proj, y_fwd, d_skip, gain]
        out_specs = y_spec
        out_shape = jax.ShapeDtypeStruct((t, width), BF16)
    else:
        in_specs = [
            pl.BlockSpec((CHUNK, conv_dim), chunk_rows),
            pl.BlockSpec((HALO, conv_dim), lambda b, s: (jnp.maximum(chunk_of(b, s) * hb - 1, 0), 0)),
            pl.BlockSpec((HALO, conv_dim), lambda b, s: (jnp.minimum((chunk_of(b, s) + 1) * hb, n_halo - 1), 0)),
        ] + scan_specs + [
            pl.BlockSpec((D_CONV, conv_dim), const),
            pl.BlockSpec((1, conv_dim), const),
        ] + head_specs
        args = [proj, proj, proj, small, small_t, cw, cb, bias_l, alog_l, bias_c, alog_c]
        out_specs = [y_spec, pl.BlockSpec((CHUNK, conv_dim), chunk_rows)]
        out_shape = [jax.ShapeDtypeStruct((t, width), F32), jax.ShapeDtypeStruct((t, conv_dim), BF16)]
        scratch = [pltpu.VMEM((CHUNK + 16, conv_dim), F32)] + scratch
    return pl.pallas_call(
        functools.partial(_ssd_kernel, reverse=reverse, nc=nc, n_heads=n_heads,
                          heads_per_group=heads_per_group, finalize=finalize),
        grid=(batch, nc),
        in_specs=in_specs,
        out_specs=out_specs,
        out_shape=out_shape,
        scratch_shapes=scratch,
        compiler_params=_params(("parallel", "arbitrary")),
        name="ssd_bwd" if reverse else "ssd_fwd",
    )(*args)


def _top16(s, tie_safe):
    rows, n = s.shape
    io = lax.broadcasted_iota(jnp.int32, s.shape, 0).astype(F32)
    io16 = lax.broadcasted_iota(jnp.int32, (PEER_TOPK, n), 0)
    rank = jnp.full(s.shape, float(PEER_TOPK), F32)
    vals = jnp.zeros((PEER_TOPK, n), F32)
    for i in range(PEER_TOPK):
        m = jnp.max(s, axis=0, keepdims=True)
        hit = s == m
        if tie_safe:
            hit = io == jnp.min(jnp.where(hit, io, float(rows)), axis=0, keepdims=True)
        rank = jnp.where(hit, float(i), rank)
        s = jnp.where(hit, -jnp.inf, s)
        vals = jnp.where(io16 == i, m, vals)
    return vals, rank


def _peer_gate_tables(s1, s2, tie_safe):
    v1, rank1 = _top16(s1, tie_safe)
    v2, rank2 = _top16(s2, tie_safe)
    n = v1.shape[1]
    sub8 = lax.broadcasted_iota(jnp.int32, (8, n), 0)
    blocks = [v1[0:1] + v2]
    for i in range(1, 8):
        blocks.append(jnp.where(sub8 < PEER_TOPK // (i + 1), v1[i:i + 1] + v2[0:8], -jnp.inf))
    blocks.append(v1[8:PEER_TOPK] + v2[0:1])
    _, crank = _top16(jnp.concatenate(blocks, axis=0), tie_safe)
    sel = jnp.where(crank < PEER_TOPK, 1.0, 0.0)
    in1 = rank1 < PEER_TOPK
    in2 = rank2 < PEER_TOPK
    counts = [jnp.sum(jnp.where(m, 1.0, 0.0), axis=0, keepdims=True) for m in (in1, in2)]
    counts.append(jnp.sum(sel, axis=0, keepdims=True))
    bad = sum(jnp.sum(jnp.where(c != float(PEER_TOPK), 1.0, 0.0)) for c in counts)
    e1 = jnp.exp(v1 - v1[0:1])
    e2 = jnp.exp(v2 - v2[0:1])
    tail = sel[PEER_TOPK + 56:]
    z = jnp.sum(sel[0:PEER_TOPK] * e2, axis=0, keepdims=True) + jnp.sum(tail * e1[8:], axis=0, keepdims=True)
    jt = jnp.where(rank1 == 0.0, jnp.sum(sel[0:PEER_TOPK], axis=0, keepdims=True), 0.0)
    for i in range(1, PEER_TOPK):
        if i < 8:
            sel_i = sel[PEER_TOPK + 8 * (i - 1):PEER_TOPK + 8 * i]
            count = jnp.sum(sel_i, axis=0, keepdims=True)
            z = z + e1[i:i + 1] * jnp.sum(sel_i * e2[0:8], axis=0, keepdims=True)
        else:
            count = tail[i - 8:i - 7]
        jt = jnp.where(rank1 == float(i), count, jt)
    a1 = jnp.where(in1, jnp.exp(s1 - v1[0:1]), 0.0)
    a2 = jnp.where(in2, jnp.exp(s2 - v2[0:1]) / z, 0.0)
    return (a1, jt, a2, rank2), bad


def _peer_topk_kernel(q_ref, keys_ref, a1_ref, jt_ref, a2_ref, r2_ref):
    half = q_ref.shape[1] // 2
    nt = (((1,), (1,)), ((), ()))
    s1 = lax.dot_general(keys_ref[0, 0], q_ref[:, :half], nt, preferred_element_type=F32)
    s2 = lax.dot_general(keys_ref[0, 1], q_ref[:, half:], nt, preferred_element_type=F32)

    def emit(tables):
        for ref, val in zip((a1_ref, jt_ref, a2_ref, r2_ref), tables):
            ref[0] = val.astype(ref.dtype)

    tables, bad = _peer_gate_tables(s1, s2, tie_safe=False)
    emit(tables)

    @pl.when(bad > 0.0)
    def _():
        emit(_peer_gate_tables(s1, s2, tie_safe=True)[0])


def _peer_topk(q, keys, *, tm=512):
    t = q.shape[0]
    n_heads, _, n_keys, half = keys.shape
    tm = min(tm, t)
    out_spec = pl.BlockSpec((1, n_keys, tm), lambda i, h: (h, 0, i))
    out_sds = [jax.ShapeDtypeStruct((n_heads, n_keys, t), dt) for dt in (F32, F32, BF16, BF16)]
    return pl.pallas_call(
        _peer_topk_kernel,
        grid=(t // tm, n_heads),
        in_specs=[
            pl.BlockSpec((tm, 2 * half), lambda i, h: (i, h)),
            pl.BlockSpec((1, 2, n_keys, half), lambda i, h: (h, 0, 0, 0)),
        ],
        out_specs=[out_spec] * 4,
        out_shape=out_sds,
        compiler_params=_params(("parallel", "parallel")),
        name="peer_topk",
    )(q, keys)


def _gelu_tanh(x):
    return 0.5 * x * (1.0 + jnp.tanh(0.7978845608028654 * (x + 0.044715 * (x * x * x))))


PEER_SUB = 256
PEER_GATE_LANES = 256
PEER_KC = 256
PEER_NC = 512


def _peer_expert_kernel(hnt_ref, h_hbm, u_ref, v_ref, a1_ref, jt_ref, a2_ref, r2_ref, o_hbm, acc_ref,
                        gate_even, gate_odd, sem, *, n_heads, rows):
    i = pl.program_id(0)
    j = pl.program_id(1)
    nj = pl.num_programs(1)
    tm = acc_ref.shape[0]
    d = u_ref.shape[1]
    n_keys = r2_ref.shape[1]
    tok = pl.ds(pl.multiple_of(i * tm, tm), tm)

    def build_gate_tile(gate_ref, tile, r, ts):
        e1 = tile * rows + r
        g = None
        for h in range(n_heads):
            thr = jt_ref[h, pl.ds(e1, 1), :][:, ts].astype(BF16)
            a1 = a1_ref[h, pl.ds(e1, 1), :][:, ts].astype(BF16)
            term = jnp.where(r2_ref[h, :, ts] < thr, a2_ref[h, :, ts], jnp.zeros((), BF16)) * a1
            g = term if g is None else g + term
        gate_ref[n_keys * r:n_keys * (r + 1), ts] = g

    @pl.when(j == 0)
    def _():
        cp = pltpu.make_async_copy(h_hbm.at[tok], acc_ref, sem)
        cp.start()
        for r in range(rows):
            build_gate_tile(gate_even, 0, r, slice(None))
        cp.wait()

    def step(gate_cur, gate_nxt):
        sub = min(PEER_SUB, tm)
        cols = [slice(sub * c, sub * (c + 1)) for c in range(tm // sub)]
        nxt = (j + 1) % nj
        gblk = min(PEER_GATE_LANES, tm)
        gate_jobs = [(r, cb) for r in range(rows) for cb in range(tm // gblk)]
        per_piece = -(-len(gate_jobs) // (len(cols) * (d // PEER_KC)))

        def next_gate_tiles(count):
            for _ in range(count):
                if gate_jobs:
                    r, cb = gate_jobs.pop(0)
                    build_gate_tile(gate_nxt, nxt, r, slice(gblk * cb, gblk * (cb + 1)))

        acts = []
        for cs in cols:
            act = None
            for k0 in range(0, d, PEER_KC):
                part = jnp.dot(u_ref[:, k0:k0 + PEER_KC], hnt_ref[k0:k0 + PEER_KC, cs], preferred_element_type=F32)
                act = part if act is None else act + part
                next_gate_tiles(per_piece)
            acts.append(act)
        next_gate_tiles(len(gate_jobs))
        for cs, act in zip(cols, acts):
            ga = gate_cur[:, cs] * _gelu_tanh(act).astype(BF16)
            for n0 in range(0, d, PEER_NC):
                acc_ref[cs, n0:n0 + PEER_NC] += lax.dot_general(
                    ga, v_ref[:, n0:n0 + PEER_NC], (((0,), (0,)), ((), ())), preferred_element_type=F32)

    @pl.when(j % 2 == 0)
    def _():
        step(gate_even, gate_odd)

    @pl.when(j % 2 == 1)
    def _():
        step(gate_odd, gate_even)

    @pl.when(j == nj - 1)
    def _():
        cp = pltpu.make_async_copy(acc_ref, o_hbm.at[tok], sem)
        cp.start()
        cp.wait()


def _peer_experts(hnt, h, eu, ev, a1, jt, a2, r2, *, tm=512, te=512):
    d, t = hnt.shape
    n_heads, n_keys, _ = a1.shape
    n_exp = eu.shape[0]
    tm = min(tm, t)
    rows = te // n_keys
    once = pl.Buffered(1)
    tok = lambda i, j: (0, 0, i)
    return pl.pallas_call(
        functools.partial(_peer_expert_kernel, n_heads=n_heads, rows=rows),
        grid=(t // tm, n_exp // te),
        in_specs=[
            pl.BlockSpec((d, tm), lambda i, j: (0, i), pipeline_mode=once),
            pl.BlockSpec(memory_space=pl.ANY),
            pl.BlockSpec((te, d), lambda i, j: (j, 0)),
            pl.BlockSpec((te, d), lambda i, j: (j, 0)),
            pl.BlockSpec((n_heads, n_keys, tm), tok, pipeline_mode=once),
            pl.BlockSpec((n_heads, n_keys, tm), tok, pipeline_mode=once),
            pl.BlockSpec((n_heads, n_keys, tm), tok, pipeline_mode=once),
            pl.BlockSpec((n_heads, n_keys, tm), tok, pipeline_mode=once),
        ],
        out_specs=pl.BlockSpec(memory_space=pl.ANY),
        out_shape=jax.ShapeDtypeStruct((t, d), F32),
        scratch_shapes=[pltpu.VMEM((tm, d), F32), pltpu.VMEM((te, tm), BF16), pltpu.VMEM((te, tm), BF16),
                        pltpu.SemaphoreType.DMA(())],
        compiler_params=_params(("arbitrary", "arbitrary")),
        name="peer_experts",
    )(hnt, h, eu, ev, a1, jt, a2, r2)


def _layer(h, pos_col, p, *, batch, seq):
    t, d = h.shape
    n_mla = p["w_uq"].shape[1] // QK_HEAD_DIM
    q_lora, kv_lora = p["w_uq"].shape[0], p["w_ukv"].shape[0]
    n_ssm = p["a_log_fwd"].shape[0]
    ssm_width = n_ssm * SSM_HEAD_DIM
    conv_dim = p["conv_b"].shape[0]
    n_groups = (conv_dim - ssm_width) // (2 * D_STATE)
    hpg = n_ssm // n_groups

    sizes = (q_lora, kv_lora, QK_ROPE_DIM, ssm_width, conv_dim, n_ssm, n_ssm)
    offs = [0]
    for s_ in sizes:
        offs.append(offs[-1] + s_)
    w_in = p["w_in"]
    seg = lambda k: w_in[:, offs[k]:offs[k + 1]]
    w_main = jnp.concatenate([seg(4), seg(3), seg(0), seg(1)], axis=1).astype(BF16)
    small_pad = LANES - QK_ROPE_DIM - 2 * n_ssm
    w_small = jnp.concatenate([seg(2), seg(5), seg(6), jnp.zeros((d, small_pad), F32)], axis=1).astype(BF16)
    z_blk = conv_dim // ssm_width
    cq_blk = (conv_dim + ssm_width) // q_lora
    ckv_blk = (conv_dim + ssm_width + q_lora) // kv_lora

    wq = p["w_uq"].reshape(q_lora, n_mla, QK_HEAD_DIM)
    wq = jnp.pad(wq, ((0, 0), (0, 0), (0, QK_PAD_DIM - QK_HEAD_DIM))).reshape(q_lora, n_mla * QK_PAD_DIM).astype(BF16)
    wkv = p["w_ukv"].reshape(kv_lora, n_mla, 2, QK_NOPE_DIM).transpose(0, 2, 1, 3).reshape(kv_lora, -1).astype(BF16)
    pad_gain = lambda g: jnp.pad(g, (0, QK_PAD_DIM - QK_HEAD_DIM)).reshape(1, QK_PAD_DIM)
    half = QK_ROPE_DIM // 2
    invf = ROPE_BASE ** (-jnp.arange(half, dtype=F32) / half)
    invf = jnp.concatenate([invf, invf, jnp.zeros((LANES - QK_ROPE_DIM,), F32)]).reshape(1, LANES)

    def head_vec(fwd, bwd):
        v = jnp.concatenate([jnp.zeros((QK_ROPE_DIM,), F32), fwd, bwd, jnp.zeros((small_pad,), F32)])
        return v.reshape(1, LANES), v.reshape(LANES, 1)

    bias_l, bias_c = head_vec(p["dt_bias_fwd"], p["dt_bias_bwd"])
    alog_l, alog_c = head_vec(p["a_log_fwd"], p["a_log_bwd"])

    xn = _rmsnorm(h, p["norm_mix"], BF16)
    proj = _matmul([xn], [w_main], BF16, name="in_proj")
    small = _matmul([xn], [w_small], F32, name="in_proj_small")
    small_t = small.T

    q, k, v = _mla_prep(proj, small, pos_col, invf, p["q_a_norm"].reshape(1, -1), p["kv_a_norm"].reshape(1, -1),
                        pad_gain(p["q_norm"]), pad_gain(p["k_norm"]), wq, wkv,
                        cq_blk=cq_blk, ckv_blk=ckv_blk, n_heads=n_mla)
    attn = _attention(q, k, v, p["attn_out_norm"].reshape(n_mla, 1, V_HEAD_DIM), batch=batch, seq=seq, n_heads=n_mla)

    cw = p["conv_w"].reshape(D_CONV, conv_dim)
    cb = p["conv_b"].reshape(1, conv_dim)
    ssd_kw = dict(batch=batch, seq=seq, n_heads=n_ssm, heads_per_group=hpg, conv_dim=conv_dim)
    y_fwd, xbc = _ssd_pass(proj, small, small_t, cw, cb, bias_l, alog_l, bias_c, alog_c, reverse=False, **ssd_kw)
    d_skip = jnp.repeat(p["d_skip"], SSM_HEAD_DIM).reshape(1, ssm_width)
    ssm = _ssd_pass(proj, small, small_t, cw, cb, bias_l, alog_l, bias_c, alog_c, reverse=True,
                    final_args=(xbc, y_fwd, d_skip, p["ssm_out_norm"].reshape(1, ssm_width)), z_blk=z_blk,
                    **ssd_kw)

    w_out = p["w_out"].astype(BF16)
    mla_width = n_mla * V_HEAD_DIM
    h = _matmul([attn, ssm], [w_out[:mla_width], w_out[mla_width:]], F32, addend=h, name="out_proj")

    hn, hnt = _rmsnorm(h, p["norm_ffn"], BF16, with_transpose=True)
    pq = _matmul([hn], [p["w_query"].astype(BF16)], BF16, name="peer_query")
    a1, jt, a2, r2 = _peer_topk(pq, p["sub_keys"].astype(BF16))
    return _peer_experts(hnt, h, p["expert_u"].astype(BF16), p["expert_v"].astype(BF16), a1, jt, a2, r2)


_PARAM_NAMES = ("norm_mix", "w_in", "q_a_norm", "w_uq", "kv_a_norm", "w_ukv", "q_norm", "k_norm", "attn_out_norm",
                "conv_w", "conv_b", "a_log_fwd", "a_log_bwd", "dt_bias_fwd", "dt_bias_bwd", "d_skip",
                "ssm_out_norm", "w_out", "norm_ffn", "w_query", "sub_keys", "expert_u", "expert_v")


def kernel(x, positions, norm_mix, w_in, q_a_norm, w_uq, kv_a_norm, w_ukv, q_norm, k_norm, attn_out_norm, conv_w, conv_b, a_log_fwd, a_log_bwd, dt_bias_fwd, dt_bias_bwd, d_skip, ssm_out_norm, w_out, norm_ffn, w_query, sub_keys, expert_u, expert_v):
    weights = (norm_mix, w_in, q_a_norm, w_uq, kv_a_norm, w_ukv, q_norm, k_norm, attn_out_norm, conv_w, conv_b,
               a_log_fwd, a_log_bwd, dt_bias_fwd, dt_bias_bwd, d_skip, ssm_out_norm, w_out, norm_ffn, w_query,
               sub_keys, expert_u, expert_v)
    batch, seq, d = x.shape
    h = x.reshape(batch * seq, d)
    pos_col = positions.reshape(batch * seq, 1).astype(F32)
    for layer in range(norm_mix.shape[0]):
        p = {name: w[layer] for name, w in zip(_PARAM_NAMES, weights)}
        h = _layer(h, pos_col, p, batch=batch, seq=seq)
    return h.reshape(batch, seq, d).astype(x.dtype)
```

```python
import functools

import jax
import jax.numpy as jnp
from jax import lax
from jax.experimental import pallas as pl
from jax.experimental.pallas import tpu as pltpu

F32 = jnp.float32
BF16 = jnp.bfloat16

EPS = 1e-6
ROPE_BASE = 10000.0
QK_NOPE_DIM = 128
QK_ROPE_DIM = 64
QK_HEAD_DIM = QK_NOPE_DIM + QK_ROPE_DIM
QK_PAD_DIM = 256
V_HEAD_DIM = 128
SSM_HEAD_DIM = 64
D_STATE = 128
D_CONV = 5
CHUNK = 128
PEER_TOPK = 16
LANES = 128
HALO = 16
VMEM_LIMIT = 56 * 1024 * 1024
LOG2_E = 1.4426950408889634


def _params(sem, vmem=VMEM_LIMIT):
    return pltpu.CompilerParams(dimension_semantics=sem, vmem_limit_bytes=vmem)


def _rmsnorm_kernel(x_ref, g_ref, o_ref, *t_refs):
    x = x_ref[...].astype(F32)
    ms = jnp.mean(x * x, axis=-1, keepdims=True)
    y = x * lax.rsqrt(ms + EPS) * g_ref[...]
    o_ref[...] = y.astype(o_ref.dtype)
    for t_ref in t_refs:
        t_ref[...] = y.T.astype(t_ref.dtype)


def _rmsnorm(x, gain, out_dtype, tm=256, with_transpose=False):
    t, d = x.shape
    tm = min(tm, t)
    out_specs = [pl.BlockSpec((tm, d), lambda i: (i, 0))]
    out_shape = [jax.ShapeDtypeStruct((t, d), out_dtype)]
    if with_transpose:
        out_specs.append(pl.BlockSpec((d, tm), lambda i: (0, i)))
        out_shape.append(jax.ShapeDtypeStruct((d, t), out_dtype))
    out = pl.pallas_call(
        _rmsnorm_kernel,
        grid=(t // tm,),
        in_specs=[pl.BlockSpec((tm, d), lambda i: (i, 0)), pl.BlockSpec((1, d), lambda i: (0, 0))],
        out_specs=out_specs,
        out_shape=out_shape,
        compiler_params=_params(("parallel",)),
        name="rmsnorm_t" if with_transpose else "rmsnorm",
    )(x, gain.reshape(1, d).astype(F32))
    return out if with_transpose else out[0]


def _matmul_kernel(*refs, n_pairs, has_add):
    o_ref = refs[-1]
    acc = None
    for p in range(n_pairs):
        part = jnp.dot(refs[p][...], refs[n_pairs + p][...], preferred_element_type=F32)
        acc = part if acc is None else acc + part
    if has_add:
        acc = acc + refs[2 * n_pairs][...].astype(F32)
    o_ref[...] = acc.astype(o_ref.dtype)


def _matmul(a_list, b_list, out_dtype, addend=None, tm=1024, tn=512, name="matmul"):
    m = a_list[0].shape[0]
    b_list = [b if isinstance(b, tuple) else (b, 0) for b in b_list]
    n = b_list[0][0].shape[1]
    tm = min(tm, m)
    tn = min(tn, n)
    while n % tn:
        tn -= LANES
    in_specs = [pl.BlockSpec((tm, a.shape[1]), lambda i, j: (i, 0)) for a in a_list]
    in_specs += [pl.BlockSpec((a.shape[1], tn), lambda i, j, rb=rb: (rb, j)) for a, (_, rb) in zip(a_list, b_list)]
    args = list(a_list) + [b for b, _ in b_list]
    if addend is not None:
        in_specs.append(pl.BlockSpec((tm, tn), lambda i, j: (i, j)))
        args.append(addend)
    return pl.pallas_call(
        functools.partial(_matmul_kernel, n_pairs=len(a_list), has_add=addend is not None),
        grid=(m // tm, n // tn),
        in_specs=in_specs,
        out_specs=pl.BlockSpec((tm, tn), lambda i, j: (i, j)),
        out_shape=jax.ShapeDtypeStruct((m, n), out_dtype),
        compiler_params=_params(("parallel", "parallel")),
        name=name,
    )(*args)


def _relayout_kernel(src_ref, *out_refs, plans):
    for o_ref, plan in zip(out_refs, plans):
        filled = 0
        for src0, dst0, width in plan:
            o_ref[:, dst0:dst0 + width] = src_ref[:, src0:src0 + width].astype(o_ref.dtype)
            filled = max(filled, dst0 + width)
        if filled < o_ref.shape[1]:
            o_ref[:, filled:] = jnp.zeros((o_ref.shape[0], o_ref.shape[1] - filled), o_ref.dtype)


def _relayout_cast(w, plans, widths, out_dtype, tr=256):
    rows, cols = w.shape
    tr = min(tr, rows)
    return pl.pallas_call(
        functools.partial(_relayout_kernel, plans=plans),
        grid=(rows // tr,),
        in_specs=[pl.BlockSpec((tr, cols), lambda i: (i, 0))],
        out_specs=[pl.BlockSpec((tr, wd), lambda i: (i, 0)) for wd in widths],
        out_shape=[jax.ShapeDtypeStruct((rows, wd), out_dtype) for wd in widths],
        compiler_params=_params(("parallel",)),
        name="w_in_relayout",
    )(w)


def _rope_tile(v, cos_t, sin_lo, sin_hi):
    return v * cos_t + pltpu.roll(v, LANES - 32, 1) * sin_lo + pltpu.roll(v, 32, 1) * sin_hi


def _mla_prep_kernel(cq_ref, ckv_ref, sm_ref, pos_ref, invf_ref, gqa_ref, gkva_ref, gq_ref, gk_ref,
                     wq_ref, wkv_ref, q_ref, k_ref, v_ref, *, n_heads, scale):
    tm = cq_ref.shape[0]
    lane = lax.broadcasted_iota(jnp.int32, (tm, LANES), 1)
    ang = pos_ref[...] * invf_ref[...]
    cos_t = jnp.where(lane < QK_ROPE_DIM, jnp.cos(ang), 0.0)
    sin_a = jnp.sin(ang)
    sin_lo = jnp.where(lane < 32, -sin_a, 0.0)
    sin_hi = jnp.where((lane >= 32) & (lane < QK_ROPE_DIM), sin_a, 0.0)

    def norm_rows(x, g):
        ms = jnp.mean(x * x, axis=-1, keepdims=True)
        return (x * lax.rsqrt(ms + EPS) * g).astype(BF16)

    cq = norm_rows(cq_ref[...].astype(F32), gqa_ref[...])
    ckv = norm_rows(ckv_ref[...].astype(F32), gkva_ref[...])
    qf = jnp.dot(cq, wq_ref[...], preferred_element_type=F32)
    kvf = jnp.dot(ckv, wkv_ref[...], preferred_element_type=F32)

    gq = gq_ref[...]
    gk = gk_ref[...]
    kr = jnp.where(lane < QK_ROPE_DIM, sm_ref[...], 0.0)
    kr_ss = jnp.sum(kr * kr, axis=-1, keepdims=True)
    kr_rot = _rope_tile(kr * gk[:, QK_NOPE_DIM:], cos_t, sin_lo, sin_hi)
    inv_dim = 1.0 / QK_HEAD_DIM
    for h in range(n_heads):
        qn = qf[:, QK_PAD_DIM * h:QK_PAD_DIM * h + QK_NOPE_DIM]
        qr = qf[:, QK_PAD_DIM * h + QK_NOPE_DIM:QK_PAD_DIM * (h + 1)]
        ss = jnp.sum(qn * qn, axis=-1, keepdims=True) + jnp.sum(qr * qr, axis=-1, keepdims=True)
        sc = lax.rsqrt(ss * inv_dim + EPS) * scale
        q_ref[:, QK_PAD_DIM * h:QK_PAD_DIM * h + QK_NOPE_DIM] = (qn * sc * gq[:, :QK_NOPE_DIM]).astype(BF16)
        q_ref[:, QK_PAD_DIM * h + QK_NOPE_DIM:QK_PAD_DIM * (h + 1)] = (
            _rope_tile(qr * gq[:, QK_NOPE_DIM:], cos_t, sin_lo, sin_hi) * sc).astype(BF16)
        kn = kvf[:, QK_NOPE_DIM * h:QK_NOPE_DIM * (h + 1)]
        ssk = jnp.sum(kn * kn, axis=-1, keepdims=True) + kr_ss
        sck = lax.rsqrt(ssk * inv_dim + EPS)
        k_ref[:, QK_PAD_DIM * h:QK_PAD_DIM * h + QK_NOPE_DIM] = (kn * sck * gk[:, :QK_NOPE_DIM]).astype(BF16)
        k_ref[:, QK_PAD_DIM * h + QK_NOPE_DIM:QK_PAD_DIM * (h + 1)] = (kr_rot * sck).astype(BF16)
    v_ref[...] = kvf[:, n_heads * QK_NOPE_DIM:].astype(BF16)


def _mla_prep(proj, small, pos_col, invf, gqa, gkva, gq, gk, wq, wkv, *, cq_blk, ckv_blk, n_heads, tm=256):
    t = proj.shape[0]
    tm = min(tm, t)
    q_lora, kv_lora = wq.shape[0], wkv.shape[0]
    const = lambda i: (0, 0)
    return pl.pallas_call(
        functools.partial(_mla_prep_kernel, n_heads=n_heads, scale=QK_HEAD_DIM ** -0.5 * LOG2_E),
        grid=(t // tm,),
        in_specs=[
            pl.BlockSpec((tm, q_lora), lambda i: (i, cq_blk)),
            pl.BlockSpec((tm, kv_lora), lambda i: (i, ckv_blk)),
            pl.BlockSpec((tm, LANES), lambda i: (i, 0)),
            pl.BlockSpec((tm, 1), lambda i: (i, 0)),
            pl.BlockSpec((1, LANES), const),
            pl.BlockSpec((1, q_lora), const),
            pl.BlockSpec((1, kv_lora), const),
            pl.BlockSpec((1, QK_PAD_DIM), const),
            pl.BlockSpec((1, QK_PAD_DIM), const),
            pl.BlockSpec(wq.shape, const),
            pl.BlockSpec(wkv.shape, const),
        ],
        out_specs=[
            pl.BlockSpec((tm, n_heads * QK_PAD_DIM), lambda i: (i, 0)),
            pl.BlockSpec((tm, n_heads * QK_PAD_DIM), lambda i: (i, 0)),
            pl.BlockSpec((tm, n_heads * V_HEAD_DIM), lambda i: (i, 0)),
        ],
        out_shape=[
            jax.ShapeDtypeStruct((t, n_heads * QK_PAD_DIM), BF16),
            jax.ShapeDtypeStruct((t, n_heads * QK_PAD_DIM), BF16),
            jax.ShapeDtypeStruct((t, n_heads * V_HEAD_DIM), BF16),
        ],
        compiler_params=_params(("parallel",)),
        name="mla_prep",
    )(proj, proj, small, pos_col, invf, gqa, gkva, gq, gk, wq, wkv)


ATTN_SUB = 256


def _attn_kernel(q_ref, k_ref, v_ref, g_ref, o_ref, vaug_s):
    @pl.when(pl.program_id(2) == 0)
    def _():
        vaug_s[:, :V_HEAD_DIM] = v_ref[...]
        vaug_s[:, V_HEAD_DIM:] = jnp.ones((vaug_s.shape[0], V_HEAD_DIM), vaug_s.dtype)

    tq = q_ref.shape[0]
    sub = min(ATTN_SUB, tq)
    for r in range(tq // sub):
        rows = slice(sub * r, sub * (r + 1))
        s = lax.dot_general(q_ref[rows], k_ref[...], (((1,), (1,)), ((), ())), preferred_element_type=F32)
        m = jnp.max(s, axis=-1, keepdims=True)
        p = jnp.exp2(s - m).astype(BF16)
        oa = jnp.dot(p, vaug_s[...], preferred_element_type=F32)
        o = oa[:, :V_HEAD_DIM] / oa[:, V_HEAD_DIM:]
        ms = jnp.mean(o * o, axis=-1, keepdims=True)
        o_ref[rows] = (o * lax.rsqrt(ms + EPS) * g_ref[0]).astype(o_ref.dtype)


def _attention(q, k, v, gain, *, batch, seq, n_heads, tq=2048):
    t = q.shape[0]
    tq = min(tq, seq)
    nq = seq // tq
    return pl.pallas_call(
        _attn_kernel,
        grid=(batch, n_heads, nq),
        in_specs=[
            pl.BlockSpec((tq, QK_PAD_DIM), lambda b, h, i: (b * nq + i, h)),
            pl.BlockSpec((seq, QK_PAD_DIM), lambda b, h, i: (b, h)),
            pl.BlockSpec((seq, V_HEAD_DIM), lambda b, h, i: (b, h)),
            pl.BlockSpec((1, 1, V_HEAD_DIM), lambda b, h, i: (h, 0, 0)),
        ],
        out_specs=pl.BlockSpec((tq, V_HEAD_DIM), lambda b, h, i: (b * nq + i, h)),
        out_shape=jax.ShapeDtypeStruct((t, n_heads * V_HEAD_DIM), BF16),
        scratch_shapes=[pltpu.VMEM((seq, 2 * V_HEAD_DIM), BF16)],
        compiler_params=_params(("parallel", "parallel", "arbitrary")),
        name="attention",
    )(q, k, v, gain)


def _softplus(x):
    return jnp.maximum(x, 0.0) + jnp.log1p(jnp.exp(-jnp.abs(x)))


def _silu(x):
    return x * jax.nn.sigmoid(x)


def _ssd_kernel(*refs, reverse, nc, n_heads, heads_per_group, finalize):
    if finalize:
        (xbc_ref, sm_ref, smt_ref, bl_ref, al_ref, bc_ref, ac_ref,
         z_ref, yf_ref, dsk_ref, gn_ref, o_ref, y_s, state_s) = refs
    else:
        (xc_ref, xp_ref, xn_ref, sm_ref, smt_ref, cw_ref, cb_ref, bl_ref, al_ref, bc_ref, ac_ref,
         o_ref, xbc_ref, ext_s, y_s, state_s) = refs
    step = pl.program_id(1)
    c = (nc - 1 - step) if reverse else step
    width = n_heads * SSM_HEAD_DIM
    n_groups = n_heads // heads_per_group

    @pl.when(step == 0)
    def _():
        state_s[...] = jnp.zeros_like(state_s)

    if not finalize:
        conv_dim = xc_ref.shape[1]
        keep_prev = jnp.where(c == 0, 0.0, 1.0)
        keep_next = jnp.where(c == nc - 1, 0.0, 1.0)
        ext_s[0:8, :] = xp_ref[...].astype(F32)[HALO - 8:] * keep_prev
        ext_s[8:8 + CHUNK, :] = xc_ref[...].astype(F32)
        ext_s[8 + CHUNK:, :] = xn_ref[...].astype(F32)[:8] * keep_next
        cblk = 512 if conv_dim % 512 == 0 else LANES
        for j in range(conv_dim // cblk):
            cs = slice(cblk * j, cblk * (j + 1))
            acc = cb_ref[:, cs]
            for kk in range(D_CONV):
                off = 8 - D_CONV // 2 + kk
                acc = acc + cw_ref[kk:kk + 1, cs] * ext_s[off:off + CHUNK, cs]
            xbc_ref[:, cs] = _silu(acc).astype(xbc_ref.dtype)

    lane0 = QK_ROPE_DIM + (n_heads if reverse else 0)
    io_r = lax.broadcasted_iota(jnp.int32, (CHUNK, CHUNK), 0)
    io_c = lax.broadcasted_iota(jnp.int32, (CHUNK, CHUNK), 1)
    tri = (io_c >= io_r) if reverse else (io_c <= io_r)
    tri_t = (io_r >= io_c) if reverse else (io_r <= io_c)
    dtl = _softplus(sm_ref[...] + bl_ref[...])
    adt = dtl * (-jnp.exp(al_ref[...]))
    cum = jnp.dot(tri.astype(F32), adt, preferred_element_type=F32, precision=lax.Precision.HIGHEST)
    dtt = _softplus(smt_ref[...] + bc_ref[...])
    adtt = dtt * (-jnp.exp(ac_ref[...]))
    cum_t = jnp.dot(adtt, tri_t.astype(F32), preferred_element_type=F32, precision=lax.Precision.HIGHEST)
    end = 0 if reverse else CHUNK - 1
    total = cum[end:end + 1, :]
    to_end = jnp.exp(total - cum)
    from_start = jnp.exp(cum)
    chunk_decay = jnp.exp(total)

    lane = lax.broadcasted_iota(jnp.int32, (CHUNK, LANES), 1)
    lo_half = lane < SSM_HEAD_DIM

    def pair_cols(arr, h0):
        a = jnp.broadcast_to(arr[:, lane0 + h0:lane0 + h0 + 1], (arr.shape[0], LANES))
        b = jnp.broadcast_to(arr[:, lane0 + h0 + 1:lane0 + h0 + 2], (arr.shape[0], LANES))
        return jnp.where(lo_half[:arr.shape[0]], a, b)

    for g in range(n_groups):
        b_g = xbc_ref[:, width + D_STATE * g:width + D_STATE * (g + 1)]
        c_g = xbc_ref[:, width + D_STATE * (n_groups + g):width + D_STATE * (n_groups + g + 1)]
        cb = lax.dot_general(c_g, b_g, (((1,), (1,)), ((), ())), preferred_element_type=F32)
        for pp in range(heads_per_group // 2):
            h0 = g * heads_per_group + 2 * pp
            q = h0 // 2
            lg = []
            for e in range(2):
                col = cum[:, lane0 + h0 + e:lane0 + h0 + e + 1]
                row = cum_t[lane0 + h0 + e:lane0 + h0 + e + 1, :]
                seg = jnp.where(tri, col - row, -jnp.inf)
                lg.append((jnp.exp(seg) * cb).astype(BF16))
            lg = jnp.concatenate(lg, axis=1)
            xd = xbc_ref[:, LANES * q:LANES * (q + 1)].astype(F32) * pair_cols(dtl, h0)
            xbd = jnp.concatenate([jnp.where(lo_half, xd, 0.0), jnp.where(lo_half, 0.0, xd)],
                                  axis=0).astype(BF16)
            y = jnp.dot(lg, xbd, preferred_element_type=F32)
            st = state_s[q]
            y = y + pair_cols(from_start, h0) * jnp.dot(c_g, st.astype(BF16), preferred_element_type=F32)
            xds = (xd * pair_cols(to_end, h0)).astype(BF16)
            state_s[q] = st * pair_cols(chunk_decay, h0) + lax.dot_general(
                b_g, xds, (((0,), (0,)), ((), ())), preferred_element_type=F32)
            y_s[:, LANES * q:LANES * (q + 1)] = y

    if not finalize:
        o_ref[...] = y_s[...]
    else:
        gsz = width // n_groups
        for g in range(n_groups):
            cs = slice(gsz * g, gsz * (g + 1))
            y = y_s[:, cs] + yf_ref[:, cs] + dsk_ref[:, cs] * xbc_ref[:, cs].astype(F32)
            y = y * _silu(z_ref[:, cs].astype(F32))
            ms = jnp.mean(y * y, axis=-1, keepdims=True)
            o_ref[:, cs] = (y * lax.rsqrt(ms + EPS) * gn_ref[:, cs]).astype(o_ref.dtype)


def _ssd_pass(proj, small, small_t, cw, cb, bias_l, alog_l, bias_c, alog_c, *, batch, seq, n_heads,
              heads_per_group, conv_dim, reverse, final_args=None, z_blk=None):
    t = proj.shape[0]
    nc = seq // CHUNK
    width = n_heads * SSM_HEAD_DIM
    hb = CHUNK // HALO
    n_halo = t // HALO

    def chunk_of(b, s):
        return b * nc + ((nc - 1 - s) if reverse else s)

    const = lambda b, s: (0, 0)
    chunk_rows = lambda b, s: (chunk_of(b, s), 0)
    scan_specs = [
        pl.BlockSpec((CHUNK, LANES), chunk_rows),
        pl.BlockSpec((LANES, CHUNK), lambda b, s: (0, chunk_of(b, s))),
    ]
    head_specs = [
        pl.BlockSpec((1, LANES), const),
        pl.BlockSpec((1, LANES), const),
        pl.BlockSpec((LANES, 1), const),
        pl.BlockSpec((LANES, 1), const),
    ]
    y_spec = pl.BlockSpec((CHUNK, width), chunk_rows)
    scratch = [pltpu.VMEM((CHUNK, width), F32), pltpu.VMEM((n_heads // 2, D_STATE, LANES), F32)]
    finalize = final_args is not None
    if finalize:
        xbc, y_fwd, d_skip, gain = final_args
        in_specs = [pl.BlockSpec((CHUNK, conv_dim), chunk_rows)] + scan_specs + head_specs + [
            pl.BlockSpec((CHUNK, width), lambda b, s: (chunk_of(b, s), z_blk)),
            y_spec,
            pl.BlockSpec((1, width), const),
            pl.BlockSpec((1, width), const),
        ]
        args = [xbc, small, small_t, bias_l, alog_l, bias_c, alog_c, proj, y_fwd, d_skip, gain]
        out_specs = y_spec
        out_shape = jax.ShapeDtypeStruct((t, width), BF16)
    else:
        in_specs = [
            pl.BlockSpec((CHUNK, conv_dim), chunk_rows),
            pl.BlockSpec((HALO, conv_dim), lambda b, s: (jnp.maximum(chunk_of(b, s) * hb - 1, 0), 0)),
            pl.BlockSpec((HALO, conv_dim), lambda b, s: (jnp.minimum((chunk_of(b, s) + 1) * hb, n_halo - 1), 0)),
        ] + scan_specs + [
            pl.BlockSpec((D_CONV, conv_dim), const),
            pl.BlockSpec((1, conv_dim), const),
        ] + head_specs
        args = [proj, proj, proj, small, small_t, cw, cb, bias_l, alog_l, bias_c, alog_c]
        out_specs = [y_spec, pl.BlockSpec((CHUNK, conv_dim), chunk_rows)]
        out_shape = [jax.ShapeDtypeStruct((t, width), F32), jax.ShapeDtypeStruct((t, conv_dim), BF16)]
        scratch = [pltpu.VMEM((CHUNK + 16, conv_dim), F32)] + scratch
    return pl.pallas_call(
        functools.partial(_ssd_kernel, reverse=reverse, nc=nc, n_heads=n_heads,
                          heads_per_group=heads_per_group, finalize=finalize),
        grid=(batch, nc),
        in_specs=in_specs,
        out_specs=out_specs,
        out_shape=out_shape,
        scratch_shapes=scratch,
        compiler_params=_params(("parallel", "arbitrary")),
        name="ssd_bwd" if reverse else "ssd_fwd",
    )(*args)


def _top16(s, tie_safe):
    rows, n = s.shape
    io = lax.broadcasted_iota(jnp.int32, s.shape, 0).astype(F32)
    io16 = lax.broadcasted_iota(jnp.int32, (PEER_TOPK, n), 0)
    rank = jnp.full(s.shape, float(PEER_TOPK), F32)
    vals = jnp.zeros((PEER_TOPK, n), F32)
    for i in range(PEER_TOPK):
        m = jnp.max(s, axis=0, keepdims=True)
        hit = s == m
        if tie_safe:
            hit = io == jnp.min(jnp.where(hit, io, float(rows)), axis=0, keepdims=True)
        rank = jnp.where(hit, float(i), rank)
        s = jnp.where(hit, -jnp.inf, s)
        vals = jnp.where(io16 == i, m, vals)
    return vals, rank


def _peer_gate_tables(s1, s2, tie_safe):
    v1, rank1 = _top16(s1, tie_safe)
    v2, rank2 = _top16(s2, tie_safe)
    n = v1.shape[1]
    sub8 = lax.broadcasted_iota(jnp.int32, (8, n), 0)
    blocks = [v1[0:1] + v2]
    for i in range(1, 8):
        blocks.append(jnp.where(sub8 < PEER_TOPK // (i + 1), v1[i:i + 1] + v2[0:8], -jnp.inf))
    blocks.append(v1[8:PEER_TOPK] + v2[0:1])
    _, crank = _top16(jnp.concatenate(blocks, axis=0), tie_safe)
    sel = jnp.where(crank < PEER_TOPK, 1.0, 0.0)
    in1 = rank1 < PEER_TOPK
    in2 = rank2 < PEER_TOPK
    counts = [jnp.sum(jnp.where(m, 1.0, 0.0), axis=0, keepdims=True) for m in (in1, in2)]
    counts.append(jnp.sum(sel, axis=0, keepdims=True))
    bad = sum(jnp.sum(jnp.where(c != float(PEER_TOPK), 1.0, 0.0)) for c in counts)
    e1 = jnp.exp(v1 - v1[0:1])
    e2 = jnp.exp(v2 - v2[0:1])
    tail = sel[PEER_TOPK + 56:]
    z = jnp.sum(sel[0:PEER_TOPK] * e2, axis=0, keepdims=True) + jnp.sum(tail * e1[8:], axis=0, keepdims=True)
    jt = jnp.where(rank1 == 0.0, jnp.sum(sel[0:PEER_TOPK], axis=0, keepdims=True), 0.0)
    for i in range(1, PEER_TOPK):
        if i < 8:
            sel_i = sel[PEER_TOPK + 8 * (i - 1):PEER_TOPK + 8 * i]
            count = jnp.sum(sel_i, axis=0, keepdims=True)
            z = z + e1[i:i + 1] * jnp.sum(sel_i * e2[0:8], axis=0, keepdims=True)
        else:
            count = tail[i - 8:i - 7]
        jt = jnp.where(rank1 == float(i), count, jt)
    a1 = jnp.where(in1, jnp.exp(s1 - v1[0:1]), 0.0)
    a2 = jnp.where(in2, jnp.exp(s2 - v2[0:1]) / z, 0.0)
    return (a1, jt, a2, rank2), bad


def _peer_topk_kernel(q_ref, keys_ref, a1_ref, jt_ref, a2_ref, r2_ref):
    half = q_ref.shape[1] // 2
    nt = (((1,), (1,)), ((), ()))
    s1 = lax.dot_general(keys_ref[0, 0], q_ref[:, :half], nt, preferred_element_type=F32)
    s2 = lax.dot_general(keys_ref[0, 1], q_ref[:, half:], nt, preferred_element_type=F32)

    def emit(tables):
        for ref, val in zip((a1_ref, jt_ref, a2_ref, r2_ref), tables):
            ref[0] = val.astype(ref.dtype)

    tables, bad = _peer_gate_tables(s1, s2, tie_safe=False)
    emit(tables)

    @pl.when(bad > 0.0)
    def _():
        emit(_peer_gate_tables(s1, s2, tie_safe=True)[0])


def _peer_topk(q, keys, *, tm=512):
    t = q.shape[0]
    n_heads, _, n_keys, half = keys.shape
    tm = min(tm, t)
    out_spec = pl.BlockSpec((1, n_keys, tm), lambda i, h: (h, 0, i))
    out_sds = [jax.ShapeDtypeStruct((n_heads, n_keys, t), dt) for dt in (F32, F32, BF16, BF16)]
    return pl.pallas_call(
        _peer_topk_kernel,
        grid=(t // tm, n_heads),
        in_specs=[
            pl.BlockSpec((tm, 2 * half), lambda i, h: (i, h)),
            pl.BlockSpec((1, 2, n_keys, half), lambda i, h: (h, 0, 0, 0)),
        ],
        out_specs=[out_spec] * 4,
        out_shape=out_sds,
        compiler_params=_params(("parallel", "parallel")),
        name="peer_topk",
    )(q, keys)


def _gelu_tanh(x):
    return 0.5 * x * (1.0 + jnp.tanh(0.7978845608028654 * (x + 0.044715 * (x * x * x))))


PEER_SUB = 256
PEER_GATE_LANES = 256
PEER_KC = 256
PEER_NC = 512


def _peer_expert_kernel(hnt_ref, h_hbm, u_ref, v_ref, a1_ref, jt_ref, a2_ref, r2_ref, o_hbm, acc_ref,
                        gate_even, gate_odd, sem, *, n_heads, rows):
    i = pl.program_id(0)
    j = pl.program_id(1)
    nj = pl.num_programs(1)
    tm = acc_ref.shape[0]
    d = u_ref.shape[1]
    n_keys = r2_ref.shape[1]
    tok = pl.ds(pl.multiple_of(i * tm, tm), tm)

    def build_gate_tile(gate_ref, tile, r, ts):
        e1 = tile * rows + r
        g = None
        for h in range(n_heads):
            thr = jt_ref[h, pl.ds(e1, 1), :][:, ts].astype(BF16)
            a1 = a1_ref[h, pl.ds(e1, 1), :][:, ts].astype(BF16)
            term = jnp.where(r2_ref[h, :, ts] < thr, a2_ref[h, :, ts], jnp.zeros((), BF16)) * a1
            g = term if g is None else g + term
        gate_ref[n_keys * r:n_keys * (r + 1), ts] = g

    @pl.when(j == 0)
    def _():
        cp = pltpu.make_async_copy(h_hbm.at[tok], acc_ref, sem)
        cp.start()
        for r in range(rows):
            build_gate_tile(gate_even, 0, r, slice(None))
        cp.wait()

    def step(gate_cur, gate_nxt):
        sub = min(PEER_SUB, tm)
        cols = [slice(sub * c, sub * (c + 1)) for c in range(tm // sub)]
        nxt = (j + 1) % nj
        gblk = min(PEER_GATE_LANES, tm)
        gate_jobs = [(r, cb) for r in range(rows) for cb in range(tm // gblk)]
        per_piece = -(-len(gate_jobs) // (len(cols) * (d // PEER_KC)))

        def next_gate_tiles(count):
            for _ in range(count):
                if gate_jobs:
                    r, cb = gate_jobs.pop(0)
                    build_gate_tile(gate_nxt, nxt, r, slice(gblk * cb, gblk * (cb + 1)))

        acts = []
        for cs in cols:
            act = None
            for k0 in range(0, d, PEER_KC):
                part = jnp.dot(u_ref[:, k0:k0 + PEER_KC], hnt_ref[k0:k0 + PEER_KC, cs], preferred_element_type=F32)
                act = part if act is None else act + part
                next_gate_tiles(per_piece)
            acts.append(act)
        next_gate_tiles(len(gate_jobs))
        for cs, act in zip(cols, acts):
            ga = gate_cur[:, cs] * _gelu_tanh(act).astype(BF16)
            for n0 in range(0, d, PEER_NC):
                acc_ref[cs, n0:n0 + PEER_NC] += lax.dot_general(
                    ga, v_ref[:, n0:n0 + PEER_NC], (((0,), (0,)), ((), ())), preferred_element_type=F32)

    @pl.when(j % 2 == 0)
    def _():
        step(gate_even, gate_odd)

    @pl.when(j % 2 == 1)
    def _():
        step(gate_odd, gate_even)

    @pl.when(j == nj - 1)
    def _():
        cp = pltpu.make_async_copy(acc_ref, o_hbm.at[tok], sem)
        cp.start()
        cp.wait()


def _peer_experts(hnt, h, eu, ev, a1, jt, a2, r2, *, tm=512, te=1024):
    d, t = hnt.shape
    n_heads, n_keys, _ = a1.shape
    n_exp = eu.shape[0]
    tm = min(tm, t)
    rows = te // n_keys
    once = pl.Buffered(1)
    tok = lambda i, j: (0, 0, i)
    return pl.pallas_call(
        functools.partial(_peer_expert_kernel, n_heads=n_heads, rows=rows),
        grid=(t // tm, n_exp // te),
        in_specs=[
            pl.BlockSpec((d, tm), lambda i, j: (0, i), pipeline_mode=once),
            pl.BlockSpec(memory_space=pl.ANY),
            pl.BlockSpec((te, d), lambda i, j: (j, 0)),
            pl.BlockSpec((te, d), lambda i, j: (j, 0)),
            pl.BlockSpec((n_heads, n_keys, tm), tok, pipeline_mode=once),
            pl.BlockSpec((n_heads, n_keys, tm), tok, pipeline_mode=once),
            pl.BlockSpec((n_heads, n_keys, tm), tok, pipeline_mode=once),
            pl.BlockSpec((n_heads, n_keys, tm), tok, pipeline_mode=once),
        ],
        out_specs=pl.BlockSpec(memory_space=pl.ANY),
        out_shape=jax.ShapeDtypeStruct((t, d), F32),
        scratch_shapes=[pltpu.VMEM((tm, d), F32), pltpu.VMEM((te, tm), BF16), pltpu.VMEM((te, tm), BF16),
                        pltpu.SemaphoreType.DMA(())],
        compiler_params=_params(("arbitrary", "arbitrary")),
        name="peer_experts",
    )(hnt, h, eu, ev, a1, jt, a2, r2)


def _layer(h, pos_col, p, *, batch, seq):
    t, d = h.shape
    n_mla = p["w_uq"].shape[1] // QK_HEAD_DIM
    q_lora, kv_lora = p["w_uq"].shape[0], p["w_ukv"].shape[0]
    n_ssm = p["a_log_fwd"].shape[0]
    ssm_width = n_ssm * SSM_HEAD_DIM
    conv_dim = p["conv_b"].shape[0]
    n_groups = (conv_dim - ssm_width) // (2 * D_STATE)
    hpg = n_ssm // n_groups

    sizes = (q_lora, kv_lora, QK_ROPE_DIM, ssm_width, conv_dim, n_ssm, n_ssm)
    offs = [0]
    for s_ in sizes:
        offs.append(offs[-1] + s_)
    small_pad = LANES - QK_ROPE_DIM - 2 * n_ssm

    def plan(order):
        moves, dst = [], 0
        for k in order:
            moves.append((offs[k], dst, sizes[k]))
            dst += sizes[k]
        return tuple(moves), dst

    main_plan, main_width = plan((4, 3, 0, 1))
    small_plan, _ = plan((2, 5, 6))
    w_main, w_small = _relayout_cast(p["w_in"], (main_plan, small_plan), (main_width, LANES), BF16)
    z_blk = conv_dim // ssm_width
    cq_blk = (conv_dim + ssm_width) // q_lora
    ckv_blk = (conv_dim + ssm_width + q_lora) // kv_lora

    wq = p["w_uq"].reshape(q_lora, n_mla, QK_HEAD_DIM)
    wq = jnp.pad(wq, ((0, 0), (0, 0), (0, QK_PAD_DIM - QK_HEAD_DIM))).reshape(q_lora, n_mla * QK_PAD_DIM).astype(BF16)
    wkv = p["w_ukv"].reshape(kv_lora, n_mla, 2, QK_NOPE_DIM).transpose(0, 2, 1, 3).reshape(kv_lora, -1).astype(BF16)
    pad_gain = lambda g: jnp.pad(g, (0, QK_PAD_DIM - QK_HEAD_DIM)).reshape(1, QK_PAD_DIM)
    half = QK_ROPE_DIM // 2
    invf = ROPE_BASE ** (-jnp.arange(half, dtype=F32) / half)
    invf = jnp.concatenate([invf, invf, jnp.zeros((LANES - QK_ROPE_DIM,), F32)]).reshape(1, LANES)

    def head_vec(fwd, bwd):
        v = jnp.concatenate([jnp.zeros((QK_ROPE_DIM,), F32), fwd, bwd, jnp.zeros((small_pad,), F32)])
        return v.reshape(1, LANES), v.reshape(LANES, 1)

    bias_l, bias_c = head_vec(p["dt_bias_fwd"], p["dt_bias_bwd"])
    alog_l, alog_c = head_vec(p["a_log_fwd"], p["a_log_bwd"])

    xn = _rmsnorm(h, p["norm_mix"], BF16)
    proj = _matmul([xn], [w_main], BF16, name="in_proj")
    small = _matmul([xn], [w_small], F32, name="in_proj_small")
    small_t = small.T

    q, k, v = _mla_prep(proj, small, pos_col, invf, p["q_a_norm"].reshape(1, -1), p["kv_a_norm"].reshape(1, -1),
                        pad_gain(p["q_norm"]), pad_gain(p["k_norm"]), wq, wkv,
                        cq_blk=cq_blk, ckv_blk=ckv_blk, n_heads=n_mla)
    attn = _attention(q, k, v, p["attn_out_norm"].reshape(n_mla, 1, V_HEAD_DIM), batch=batch, seq=seq, n_heads=n_mla)

    cw = p["conv_w"].reshape(D_CONV, conv_dim)
    cb = p["conv_b"].reshape(1, conv_dim)
    ssd_kw = dict(batch=batch, seq=seq, n_heads=n_ssm, heads_per_group=hpg, conv_dim=conv_dim)
    y_fwd, xbc = _ssd_pass(proj, small, small_t, cw, cb, bias_l, alog_l, bias_c, alog_c, reverse=False, **ssd_kw)
    d_skip = jnp.repeat(p["d_skip"], SSM_HEAD_DIM).reshape(1, ssm_width)
    ssm = _ssd_pass(proj, small, small_t, cw, cb, bias_l, alog_l, bias_c, alog_c, reverse=True,
                    final_args=(xbc, y_fwd, d_skip, p["ssm_out_norm"].reshape(1, ssm_width)), z_blk=z_blk,
                    **ssd_kw)

    w_out = p["w_out"].astype(BF16)
    assert n_mla * V_HEAD_DIM == ssm_width, "output projection assumes two equal-width mixer halves"
    h = _matmul([attn, ssm], [(w_out, 0), (w_out, 1)], F32, addend=h, name="out_proj")

    hn, hnt = _rmsnorm(h, p["norm_ffn"], BF16, with_transpose=True)
    pq = _matmul([hn], [p["w_query"].astype(BF16)], BF16, name="peer_query")
    a1, jt, a2, r2 = _peer_topk(pq, p["sub_keys"].astype(BF16))
    return _peer_experts(hnt, h, p["expert_u"].astype(BF16), p["expert_v"].astype(BF16), a1, jt, a2, r2)


_PARAM_NAMES = ("norm_mix", "w_in", "q_a_norm", "w_uq", "kv_a_norm", "w_ukv", "q_norm", "k_norm", "attn_out_norm",
                "conv_w", "conv_b", "a_log_fwd", "a_log_bwd", "dt_bias_fwd", "dt_bias_bwd", "d_skip",
                "ssm_out_norm", "w_out", "norm_ffn", "w_query", "sub_keys", "expert_u", "expert_v")


def kernel(x, positions, norm_mix, w_in, q_a_norm, w_uq, kv_a_norm, w_ukv, q_norm, k_norm, attn_out_norm, conv_w, conv_b, a_log_fwd, a_log_bwd, dt_bias_fwd, dt_bias_bwd, d_skip, ssm_out_norm, w_out, norm_ffn, w_query, sub_keys, expert_u, expert_v):
    weights = (norm_mix, w_in, q_a_norm, w_uq, kv_a_norm, w_ukv, q_norm, k_norm, attn_out_norm, conv_w, conv_b,
               a_log_fwd, a_log_bwd, dt_bias_fwd, dt_bias_bwd, d_skip, ssm_out_norm, w_out, norm_ffn, w_query,
               sub_keys, expert_u, expert_v)
    batch, seq, d = x.shape
    h = x.reshape(batch * seq, d)
    pos_col = positions.reshape(batch * seq, 1).astype(F32)
    for layer in range(norm_mix.shape[0]):
        p = {name: w[layer] for name, w in zip(_PARAM_NAMES, weights)}
        h = _layer(h, pos_col, p, batch=batch, seq=seq)
    return h.reshape(batch, seq, d).astype(x.dtype)
```

```python
import functools

import jax
import jax.numpy as jnp
from jax import lax
from jax.experimental import pallas as pl
from jax.experimental.pallas import tpu as pltpu

F32 = jnp.float32
BF16 = jnp.bfloat16

EPS = 1e-6
ROPE_BASE = 10000.0
QK_NOPE_DIM = 128
QK_ROPE_DIM = 64
QK_HEAD_DIM = QK_NOPE_DIM + QK_ROPE_DIM
QK_PAD_DIM = 256
V_HEAD_DIM = 128
SSM_HEAD_DIM = 64
D_STATE = 128
D_CONV = 5
CHUNK = 128
PEER_TOPK = 16
LANES = 128
HALO = 16
VMEM_LIMIT = 56 * 1024 * 1024
LOG2_E = 1.4426950408889634


def _params(sem, vmem=VMEM_LIMIT):
    return pltpu.CompilerParams(dimension_semantics=sem, vmem_limit_bytes=vmem)


def _rmsnorm_kernel(x_ref, g_ref, o_ref, *t_refs):
    x = x_ref[...].astype(F32)
    ms = jnp.mean(x * x, axis=-1, keepdims=True)
    y = x * lax.rsqrt(ms + EPS) * g_ref[...]
    o_ref[...] = y.astype(o_ref.dtype)
    for t_ref in t_refs:
        t_ref[...] = y.T.astype(t_ref.dtype)


def _rmsnorm(x, gain, out_dtype, tm=256, with_transpose=False):
    t, d = x.shape
    tm = min(tm, t)
    out_specs = [pl.BlockSpec((tm, d), lambda i: (i, 0))]
    out_shape = [jax.ShapeDtypeStruct((t, d), out_dtype)]
    if with_transpose:
        out_specs.append(pl.BlockSpec((d, tm), lambda i: (0, i)))
        out_shape.append(jax.ShapeDtypeStruct((d, t), out_dtype))
    out = pl.pallas_call(
        _rmsnorm_kernel,
        grid=(t // tm,),
        in_specs=[pl.BlockSpec((tm, d), lambda i: (i, 0)), pl.BlockSpec((1, d), lambda i: (0, 0))],
        out_specs=out_specs,
        out_shape=out_shape,
        compiler_params=_params(("parallel",)),
        name="rmsnorm_t" if with_transpose else "rmsnorm",
    )(x, gain.reshape(1, d).astype(F32))
    return out if with_transpose else out[0]


def _matmul_kernel(*refs, n_pairs, has_add):
    o_ref = refs[-1]
    acc = None
    for p in range(n_pairs):
        part = jnp.dot(refs[p][...], refs[n_pairs + p][...], preferred_element_type=F32)
        acc = part if acc is None else acc + part
    if has_add:
        acc = acc + refs[2 * n_pairs][...].astype(F32)
    o_ref[...] = acc.astype(o_ref.dtype)


def _matmul(a_list, b_list, out_dtype, addend=None, tm=1024, tn=512, name="matmul"):
    m = a_list[0].shape[0]
    b_list = [b if isinstance(b, tuple) else (b, 0) for b in b_list]
    n = b_list[0][0].shape[1]
    tm = min(tm, m)
    tn = min(tn, n)
    while n % tn:
        tn -= LANES
    in_specs = [pl.BlockSpec((tm, a.shape[1]), lambda i, j: (i, 0)) for a in a_list]
    in_specs += [pl.BlockSpec((a.shape[1], tn), lambda i, j, rb=rb: (rb, j)) for a, (_, rb) in zip(a_list, b_list)]
    args = list(a_list) + [b for b, _ in b_list]
    if addend is not None:
        in_specs.append(pl.BlockSpec((tm, tn), lambda i, j: (i, j)))
        args.append(addend)
    return pl.pallas_call(
        functools.partial(_matmul_kernel, n_pairs=len(a_list), has_add=addend is not None),
        grid=(m // tm, n // tn),
        in_specs=in_specs,
        out_specs=pl.BlockSpec((tm, tn), lambda i, j: (i, j)),
        out_shape=jax.ShapeDtypeStruct((m, n), out_dtype),
        compiler_params=_params(("parallel", "parallel")),
        name=name,
    )(*args)


def _relayout_kernel(src_ref, *out_refs, plans):
    for o_ref, plan in zip(out_refs, plans):
        filled = 0
        for src0, dst0, width in plan:
            o_ref[:, dst0:dst0 + width] = src_ref[:, src0:src0 + width].astype(o_ref.dtype)
            filled = max(filled, dst0 + width)
        if filled < o_ref.shape[1]:
            o_ref[:, filled:] = jnp.zeros((o_ref.shape[0], o_ref.shape[1] - filled), o_ref.dtype)


def _relayout_cast(w, plans, widths, out_dtype, tr=256):
    rows, cols = w.shape
    tr = min(tr, rows)
    return pl.pallas_call(
        functools.partial(_relayout_kernel, plans=plans),
        grid=(rows // tr,),
        in_specs=[pl.BlockSpec((tr, cols), lambda i: (i, 0))],
        out_specs=[pl.BlockSpec((tr, wd), lambda i: (i, 0)) for wd in widths],
        out_shape=[jax.ShapeDtypeStruct((rows, wd), out_dtype) for wd in widths],
        compiler_params=_params(("parallel",)),
        name="w_in_relayout",
    )(w)


def _rope_tile(v, cos_t, sin_lo, sin_hi):
    return v * cos_t + pltpu.roll(v, LANES - 32, 1) * sin_lo + pltpu.roll(v, 32, 1) * sin_hi


def _mla_prep_kernel(cq_ref, ckv_ref, sm_ref, pos_ref, invf_ref, gqa_ref, gkva_ref, gq_ref, gk_ref,
                     wq_ref, wkv_ref, q_ref, k_ref, v_ref, *, n_heads, scale):
    tm = cq_ref.shape[0]
    lane = lax.broadcasted_iota(jnp.int32, (tm, LANES), 1)
    ang = pos_ref[...] * invf_ref[...]
    cos_t = jnp.where(lane < QK_ROPE_DIM, jnp.cos(ang), 0.0)
    sin_a = jnp.sin(ang)
    sin_lo = jnp.where(lane < 32, -sin_a, 0.0)
    sin_hi = jnp.where((lane >= 32) & (lane < QK_ROPE_DIM), sin_a, 0.0)

    def norm_rows(x, g):
        ms = jnp.mean(x * x, axis=-1, keepdims=True)
        return (x * lax.rsqrt(ms + EPS) * g).astype(BF16)

    cq = norm_rows(cq_ref[...].astype(F32), gqa_ref[...])
    ckv = norm_rows(ckv_ref[...].astype(F32), gkva_ref[...])
    qf = jnp.dot(cq, wq_ref[...], preferred_element_type=F32)
    kvf = jnp.dot(ckv, wkv_ref[...], preferred_element_type=F32)

    gq = gq_ref[...]
    gk = gk_ref[...]
    kr = jnp.where(lane < QK_ROPE_DIM, sm_ref[...], 0.0)
    kr_ss = jnp.sum(kr * kr, axis=-1, keepdims=True)
    kr_rot = _rope_tile(kr * gk[:, QK_NOPE_DIM:], cos_t, sin_lo, sin_hi)
    inv_dim = 1.0 / QK_HEAD_DIM
    for h in range(n_heads):
        qn = qf[:, QK_PAD_DIM * h:QK_PAD_DIM * h + QK_NOPE_DIM]
        qr = qf[:, QK_PAD_DIM * h + QK_NOPE_DIM:QK_PAD_DIM * (h + 1)]
        ss = jnp.sum(qn * qn, axis=-1, keepdims=True) + jnp.sum(qr * qr, axis=-1, keepdims=True)
        sc = lax.rsqrt(ss * inv_dim + EPS) * scale
        q_ref[:, QK_PAD_DIM * h:QK_PAD_DIM * h + QK_NOPE_DIM] = (qn * sc * gq[:, :QK_NOPE_DIM]).astype(BF16)
        q_ref[:, QK_PAD_DIM * h + QK_NOPE_DIM:QK_PAD_DIM * (h + 1)] = (
            _rope_tile(qr * gq[:, QK_NOPE_DIM:], cos_t, sin_lo, sin_hi) * sc).astype(BF16)
        kn = kvf[:, QK_NOPE_DIM * h:QK_NOPE_DIM * (h + 1)]
        ssk = jnp.sum(kn * kn, axis=-1, keepdims=True) + kr_ss
        sck = lax.rsqrt(ssk * inv_dim + EPS)
        k_ref[:, QK_PAD_DIM * h:QK_PAD_DIM * h + QK_NOPE_DIM] = (kn * sck * gk[:, :QK_NOPE_DIM]).astype(BF16)
        k_ref[:, QK_PAD_DIM * h + QK_NOPE_DIM:QK_PAD_DIM * (h + 1)] = (kr_rot * sck).astype(BF16)
    v_ref[...] = kvf[:, n_heads * QK_NOPE_DIM:].astype(BF16)


def _mla_prep(proj, small, pos_col, invf, gqa, gkva, gq, gk, wq, wkv, *, cq_blk, ckv_blk, n_heads, tm=256):
    t = proj.shape[0]
    tm = min(tm, t)
    q_lora, kv_lora = wq.shape[0], wkv.shape[0]
    const = lambda i: (0, 0)
    return pl.pallas_call(
        functools.partial(_mla_prep_kernel, n_heads=n_heads, scale=QK_HEAD_DIM ** -0.5 * LOG2_E),
        grid=(t // tm,),
        in_specs=[
            pl.BlockSpec((tm, q_lora), lambda i: (i, cq_blk)),
            pl.BlockSpec((tm, kv_lora), lambda i: (i, ckv_blk)),
            pl.BlockSpec((tm, LANES), lambda i: (i, 0)),
            pl.BlockSpec((tm, 1), lambda i: (i, 0)),
            pl.BlockSpec((1, LANES), const),
            pl.BlockSpec((1, q_lora), const),
            pl.BlockSpec((1, kv_lora), const),
            pl.BlockSpec((1, QK_PAD_DIM), const),
            pl.BlockSpec((1, QK_PAD_DIM), const),
            pl.BlockSpec(wq.shape, const),
            pl.BlockSpec(wkv.shape, const),
        ],
        out_specs=[
            pl.BlockSpec((tm, n_heads * QK_PAD_DIM), lambda i: (i, 0)),
            pl.BlockSpec((tm, n_heads * QK_PAD_DIM), lambda i: (i, 0)),
            pl.BlockSpec((tm, n_heads * V_HEAD_DIM), lambda i: (i, 0)),
        ],
        out_shape=[
            jax.ShapeDtypeStruct((t, n_heads * QK_PAD_DIM), BF16),
            jax.ShapeDtypeStruct((t, n_heads * QK_PAD_DIM), BF16),
            jax.ShapeDtypeStruct((t, n_heads * V_HEAD_DIM), BF16),
        ],
        compiler_params=_params(("parallel",)),
        name="mla_prep",
    )(proj, proj, small, pos_col, invf, gqa, gkva, gq, gk, wq, wkv)


ATTN_SUB = 256


def _attn_kernel(q_ref, k_ref, v_ref, g_ref, o_ref, vaug_s):
    @pl.when(pl.program_id(2) == 0)
    def _():
        vaug_s[:, :V_HEAD_DIM] = v_ref[...]
        vaug_s[:, V_HEAD_DIM:] = jnp.ones((vaug_s.shape[0], V_HEAD_DIM), vaug_s.dtype)

    tq = q_ref.shape[0]
    sub = min(ATTN_SUB, tq)
    for r in range(tq // sub):
        rows = slice(sub * r, sub * (r + 1))
        s = lax.dot_general(q_ref[rows], k_ref[...], (((1,), (1,)), ((), ())), preferred_element_type=F32)
        m = jnp.max(s, axis=-1, keepdims=True)
        p = jnp.exp2(s - m).astype(BF16)
        oa = jnp.dot(p, vaug_s[...], preferred_element_type=F32)
        o = oa[:, :V_HEAD_DIM] / oa[:, V_HEAD_DIM:]
        ms = jnp.mean(o * o, axis=-1, keepdims=True)
        o_ref[rows] = (o * lax.rsqrt(ms + EPS) * g_ref[0]).astype(o_ref.dtype)


def _attention(q, k, v, gain, *, batch, seq, n_heads, tq=2048):
    t = q.shape[0]
    tq = min(tq, seq)
    nq = seq // tq
    return pl.pallas_call(
        _attn_kernel,
        grid=(batch, n_heads, nq),
        in_specs=[
            pl.BlockSpec((tq, QK_PAD_DIM), lambda b, h, i: (b * nq + i, h)),
            pl.BlockSpec((seq, QK_PAD_DIM), lambda b, h, i: (b, h)),
            pl.BlockSpec((seq, V_HEAD_DIM), lambda b, h, i: (b, h)),
            pl.BlockSpec((1, 1, V_HEAD_DIM), lambda b, h, i: (h, 0, 0)),
        ],
        out_specs=pl.BlockSpec((tq, V_HEAD_DIM), lambda b, h, i: (b * nq + i, h)),
        out_shape=jax.ShapeDtypeStruct((t, n_heads * V_HEAD_DIM), BF16),
        scratch_shapes=[pltpu.VMEM((seq, 2 * V_HEAD_DIM), BF16)],
        compiler_params=_params(("parallel", "parallel", "arbitrary")),
        name="attention",
    )(q, k, v, gain)


def _softplus(x):
    return jnp.maximum(x, 0.0) + jnp.log1p(jnp.exp(-jnp.abs(x)))


def _silu(x):
    return x * jax.nn.sigmoid(x)


def _ssd_kernel(*refs, reverse, nc, n_heads, heads_per_group, finalize):
    if finalize:
        (xbc_ref, sm_ref, smt_ref, bl_ref, al_ref, bc_ref, ac_ref,
         z_ref, yf_ref, dsk_ref, gn_ref, o_ref, y_s, state_s) = refs
    else:
        (xc_ref, xp_ref, xn_ref, sm_ref, smt_ref, cw_ref, cb_ref, bl_ref, al_ref, bc_ref, ac_ref,
         o_ref, xbc_ref, ext_s, y_s, state_s) = refs
    step = pl.program_id(1)
    c = (nc - 1 - step) if reverse else step
    width = n_heads * SSM_HEAD_DIM
    n_groups = n_heads // heads_per_group

    @pl.when(step == 0)
    def _():
        state_s[...] = jnp.zeros_like(state_s)

    if not finalize:
        conv_dim = xc_ref.shape[1]
        keep_prev = jnp.where(c == 0, 0.0, 1.0)
        keep_next = jnp.where(c == nc - 1, 0.0, 1.0)
        ext_s[0:8, :] = xp_ref[...].astype(F32)[HALO - 8:] * keep_prev
        ext_s[8:8 + CHUNK, :] = xc_ref[...].astype(F32)
        ext_s[8 + CHUNK:, :] = xn_ref[...].astype(F32)[:8] * keep_next
        cblk = 512 if conv_dim % 512 == 0 else LANES
        for j in range(conv_dim // cblk):
            cs = slice(cblk * j, cblk * (j + 1))
            acc = cb_ref[:, cs]
            for kk in range(D_CONV):
                off = 8 - D_CONV // 2 + kk
                acc = acc + cw_ref[kk:kk + 1, cs] * ext_s[off:off + CHUNK, cs]
            xbc_ref[:, cs] = _silu(acc).astype(xbc_ref.dtype)

    lane0 = QK_ROPE_DIM + (n_heads if reverse else 0)
    io_r = lax.broadcasted_iota(jnp.int32, (CHUNK, CHUNK), 0)
    io_c = lax.broadcasted_iota(jnp.int32, (CHUNK, CHUNK), 1)
    tri = (io_c >= io_r) if reverse else (io_c <= io_r)
    tri_t = (io_r >= io_c) if reverse else (io_r <= io_c)
    dtl = _softplus(sm_ref[...] + bl_ref[...])
    adt = dtl * (-jnp.exp(al_ref[...]))
    cum = jnp.dot(tri.astype(F32), adt, preferred_element_type=F32, precision=lax.Precision.HIGHEST)
    dtt = _softplus(smt_ref[...] + bc_ref[...])
    adtt = dtt * (-jnp.exp(ac_ref[...]))
    cum_t = jnp.dot(adtt, tri_t.astype(F32), preferred_element_type=F32, precision=lax.Precision.HIGHEST)
    end = 0 if reverse else CHUNK - 1
    total = cum[end:end + 1, :]
    to_end = jnp.exp(total - cum)
    from_start = jnp.exp(cum)
    chunk_decay = jnp.exp(total)

    lane = lax.broadcasted_iota(jnp.int32, (CHUNK, LANES), 1)
    lo_half = lane < SSM_HEAD_DIM

    def pair_cols(arr, h0):
        a = jnp.broadcast_to(arr[:, lane0 + h0:lane0 + h0 + 1], (arr.shape[0], LANES))
        b = jnp.broadcast_to(arr[:, lane0 + h0 + 1:lane0 + h0 + 2], (arr.shape[0], LANES))
        return jnp.where(lo_half[:arr.shape[0]], a, b)

    for g in range(n_groups):
        b_g = xbc_ref[:, width + D_STATE * g:width + D_STATE * (g + 1)]
        c_g = xbc_ref[:, width + D_STATE * (n_groups + g):width + D_STATE * (n_groups + g + 1)]
        cb = lax.dot_general(c_g, b_g, (((1,), (1,)), ((), ())), preferred_element_type=F32)
        for pp in range(heads_per_group // 2):
            h0 = g * heads_per_group + 2 * pp
            q = h0 // 2
            lg = []
            for e in range(2):
                col = cum[:, lane0 + h0 + e:lane0 + h0 + e + 1]
                row = cum_t[lane0 + h0 + e:lane0 + h0 + e + 1, :]
                seg = jnp.where(tri, col - row, -jnp.inf)
                lg.append((jnp.exp(seg) * cb).astype(BF16))
            lg = jnp.concatenate(lg, axis=1)
            xd = xbc_ref[:, LANES * q:LANES * (q + 1)].astype(F32) * pair_cols(dtl, h0)
            xbd = jnp.concatenate([jnp.where(lo_half, xd, 0.0), jnp.where(lo_half, 0.0, xd)],
                                  axis=0).astype(BF16)
            y = jnp.dot(lg, xbd, preferred_element_type=F32)
            st = state_s[q]
            y = y + pair_cols(from_start, h0) * jnp.dot(c_g, st.astype(BF16), preferred_element_type=F32)
            xds = (xd * pair_cols(to_end, h0)).astype(BF16)
            state_s[q] = st * pair_cols(chunk_decay, h0) + lax.dot_general(
                b_g, xds, (((0,), (0,)), ((), ())), preferred_element_type=F32)
            y_s[:, LANES * q:LANES * (q + 1)] = y

    if not finalize:
        o_ref[...] = y_s[...]
    else:
        gsz = width // n_groups
        for g in range(n_groups):
            cs = slice(gsz * g, gsz * (g + 1))
            y = y_s[:, cs] + yf_ref[:, cs] + dsk_ref[:, cs] * xbc_ref[:, cs].astype(F32)
            y = y * _silu(z_ref[:, cs].astype(F32))
            ms = jnp.mean(y * y, axis=-1, keepdims=True)
            o_ref[:, cs] = (y * lax.rsqrt(ms + EPS) * gn_ref[:, cs]).astype(o_ref.dtype)


def _ssd_pass(proj, small, small_t, cw, cb, bias_l, alog_l, bias_c, alog_c, *, batch, seq, n_heads,
              heads_per_group, conv_dim, reverse, final_args=None, z_blk=None):
    t = proj.shape[0]
    nc = seq // CHUNK
    width = n_heads * SSM_HEAD_DIM
    hb = CHUNK // HALO
    n_halo = t // HALO

    def chunk_of(b, s):
        return b * nc + ((nc - 1 - s) if reverse else s)

    const = lambda b, s: (0, 0)
    chunk_rows = lambda b, s: (chunk_of(b, s), 0)
    scan_specs = [
        pl.BlockSpec((CHUNK, LANES), chunk_rows),
        pl.BlockSpec((LANES, CHUNK), lambda b, s: (0, chunk_of(b, s))),
    ]
    head_specs = [
        pl.BlockSpec((1, LANES), const),
        pl.BlockSpec((1, LANES), const),
        pl.BlockSpec((LANES, 1), const),
        pl.BlockSpec((LANES, 1), const),
    ]
    y_spec = pl.BlockSpec((CHUNK, width), chunk_rows)
    scratch = [pltpu.VMEM((CHUNK, width), F32), pltpu.VMEM((n_heads // 2, D_STATE, LANES), F32)]
    finalize = final_args is not None
    if finalize:
        xbc, y_fwd, d_skip, gain = final_args
        in_specs = [pl.BlockSpec((CHUNK, conv_dim), chunk_rows)] + scan_specs + head_specs + [
            pl.BlockSpec((CHUNK, width), lambda b, s: (chunk_of(b, s), z_blk)),
            y_spec,
            pl.BlockSpec((1, width), const),
            pl.BlockSpec((1, width), const),
        ]
        args = [xbc, small, small_t, bias_l, alog_l, bias_c, alog_c, proj, y_fwd, d_skip, gain]
        out_specs = y_spec
        out_shape = jax.ShapeDtypeStruct((t, width), BF16)
    else:
        in_specs = [
            pl.BlockSpec((CHUNK, conv_dim), chunk_rows),
            pl.BlockSpec((HALO, conv_dim), lambda b, s: (jnp.maximum(chunk_of(b, s) * hb - 1, 0), 0)),
            pl.BlockSpec((HALO, conv_dim), lambda b, s: (jnp.minimum((chunk_of(b, s) + 1) * hb, n_halo - 1), 0)),
        ] + scan_specs + [
            pl.BlockSpec((D_CONV, conv_dim), const),
            pl.BlockSpec((1, conv_dim), const),
        ] + head_specs
        args = [proj, proj, proj, small, small_t, cw, cb, bias_l, alog_l, bias_c, alog_c]
        out_specs = [y_spec, pl.BlockSpec((CHUNK, conv_dim), chunk_rows)]
        out_shape = [jax.ShapeDtypeStruct((t, width), F32), jax.ShapeDtypeStruct((t, conv_dim), BF16)]
        scratch = [pltpu.VMEM((CHUNK + 16, conv_dim), F32)] + scratch
    return pl.pallas_call(
        functools.partial(_ssd_kernel, reverse=reverse, nc=nc, n_heads=n_heads,
                          heads_per_group=heads_per_group, finalize=finalize),
        grid=(batch, nc),
        in_specs=in_specs,
        out_specs=out_specs,
        out_shape=out_shape,
        scratch_shapes=scratch,
        compiler_params=_params(("parallel", "arbitrary")),
        name="ssd_bwd" if reverse else "ssd_fwd",
    )(*args)


def _top16(s, tie_safe):
    rows, n = s.shape
    io = lax.broadcasted_iota(jnp.int32, s.shape, 0).astype(F32)
    io16 = lax.broadcasted_iota(jnp.int32, (PEER_TOPK, n), 0)
    rank = jnp.full(s.shape, float(PEER_TOPK), F32)
    vals = jnp.zeros((PEER_TOPK, n), F32)
    for i in range(PEER_TOPK):
        m = jnp.max(s, axis=0, keepdims=True)
        hit = s == m
        if tie_safe:
            hit = io == jnp.min(jnp.where(hit, io, float(rows)), axis=0, keepdims=True)
        rank = jnp.where(hit, float(i), rank)
        s = jnp.where(hit, -jnp.inf, s)
        vals = jnp.where(io16 == i, m, vals)
    return vals, rank


def _peer_gate_tables(s1, s2, tie_safe):
    v1, rank1 = _top16(s1, tie_safe)
    v2, rank2 = _top16(s2, tie_safe)
    n = v1.shape[1]
    sub8 = lax.broadcasted_iota(jnp.int32, (8, n), 0)
    blocks = [v1[0:1] + v2]
    for i in range(1, 8):
        blocks.append(jnp.where(sub8 < PEER_TOPK // (i + 1), v1[i:i + 1] + v2[0:8], -jnp.inf))
    blocks.append(v1[8:PEER_TOPK] + v2[0:1])
    _, crank = _top16(jnp.concatenate(blocks, axis=0), tie_safe)
    sel = jnp.where(crank < PEER_TOPK, 1.0, 0.0)
    in1 = rank1 < PEER_TOPK
    in2 = rank2 < PEER_TOPK
    counts = [jnp.sum(jnp.where(m, 1.0, 0.0), axis=0, keepdims=True) for m in (in1, in2)]
    counts.append(jnp.sum(sel, axis=0, keepdims=True))
    bad = sum(jnp.sum(jnp.where(c != float(PEER_TOPK), 1.0, 0.0)) for c in counts)
    e1 = jnp.exp(v1 - v1[0:1])
    e2 = jnp.exp(v2 - v2[0:1])
    tail = sel[PEER_TOPK + 56:]
    z = jnp.sum(sel[0:PEER_TOPK] * e2, axis=0, keepdims=True) + jnp.sum(tail * e1[8:], axis=0, keepdims=True)
    jt = jnp.where(rank1 == 0.0, jnp.sum(sel[0:PEER_TOPK], axis=0, keepdims=True), 0.0)
    for i in range(1, PEER_TOPK):
        if i < 8:
            sel_i = sel[PEER_TOPK + 8 * (i - 1):PEER_TOPK + 8 * i]
            count = jnp.sum(sel_i, axis=0, keepdims=True)
            z = z + e1[i:i + 1] * jnp.sum(sel_i * e2[0:8], axis=0, keepdims=True)
        else:
            count = tail[i - 8:i - 7]
        jt = jnp.where(rank1 == float(i), count, jt)
    a1 = jnp.where(in1, jnp.exp(s1 - v1[0:1]), 0.0)
    a2 = jnp.where(in2, jnp.exp(s2 - v2[0:1]) / z, 0.0)
    return (a1, jt, a2, rank2), bad


def _peer_topk_kernel(q_ref, keys_ref, a1_ref, jt_ref, a2_ref, r2_ref):
    half = q_ref.shape[1] // 2
    nt = (((1,), (1,)), ((), ()))
    s1 = lax.dot_general(keys_ref[0, 0], q_ref[:, :half], nt, preferred_element_type=F32)
    s2 = lax.dot_general(keys_ref[0, 1], q_ref[:, half:], nt, preferred_element_type=F32)

    def emit(tables):
        for ref, val in zip((a1_ref, jt_ref, a2_ref, r2_ref), tables):
            ref[0] = val.astype(ref.dtype)

    tables, bad = _peer_gate_tables(s1, s2, tie_safe=False)
    emit(tables)

    @pl.when(bad > 0.0)
    def _():
        emit(_peer_gate_tables(s1, s2, tie_safe=True)[0])


def _peer_topk(q, keys, *, tm=512):
    t = q.shape[0]
    n_heads, _, n_keys, half = keys.shape
    tm = min(tm, t)
    out_spec = pl.BlockSpec((1, n_keys, tm), lambda i, h: (h, 0, i))
    out_sds = [jax.ShapeDtypeStruct((n_heads, n_keys, t), dt) for dt in (F32, F32, BF16, BF16)]
    return pl.pallas_call(
        _peer_topk_kernel,
        grid=(t // tm, n_heads),
        in_specs=[
            pl.BlockSpec((tm, 2 * half), lambda i, h: (i, h)),
            pl.BlockSpec((1, 2, n_keys, half), lambda i, h: (h, 0, 0, 0)),
        ],
        out_specs=[out_spec] * 4,
        out_shape=out_sds,
        compiler_params=_params(("parallel", "parallel")),
        name="peer_topk",
    )(q, keys)


def _gelu_tanh(x):
    return 0.5 * x * (1.0 + jnp.tanh(0.7978845608028654 * (x + 0.044715 * (x * x * x))))


PEER_SUB = 256
PEER_GATE_LANES = 256
PEER_KC = 256
PEER_NC = 512


def _peer_expert_kernel(hnt_ref, h_hbm, u_ref, v_ref, a1_ref, jt_ref, a2_ref, r2_ref, o_hbm, acc_ref,
                        gate_even, gate_odd, sem, *, n_heads, rows):
    i = pl.program_id(0)
    j = pl.program_id(1)
    nj = pl.num_programs(1)
    tm = acc_ref.shape[0]
    d = u_ref.shape[1]
    n_keys = r2_ref.shape[1]
    tok = pl.ds(pl.multiple_of(i * tm, tm), tm)

    def build_gate_tile(gate_ref, tile, r, ts):
        e1 = tile * rows + r
        g = None
        for h in range(n_heads):
            thr = jt_ref[h, pl.ds(e1, 1), :][:, ts].astype(BF16)
            a1 = a1_ref[h, pl.ds(e1, 1), :][:, ts].astype(BF16)
            term = jnp.where(r2_ref[h, :, ts] < thr, a2_ref[h, :, ts], jnp.zeros((), BF16)) * a1
            g = term if g is None else g + term
        gate_ref[n_keys * r:n_keys * (r + 1), ts] = g

    @pl.when(j == 0)
    def _():
        cp = pltpu.make_async_copy(h_hbm.at[tok], acc_ref, sem)
        cp.start()
        for r in range(rows):
            build_gate_tile(gate_even, 0, r, slice(None))
        cp.wait()

    def step(gate_cur, gate_nxt):
        sub = min(PEER_SUB, tm)
        cols = [slice(sub * c, sub * (c + 1)) for c in range(tm // sub)]
        nxt = (j + 1) % nj
        gblk = min(PEER_GATE_LANES, tm)
        gate_jobs = [(r, cb) for r in range(rows) for cb in range(tm // gblk)]
        per_piece = -(-len(gate_jobs) // (len(cols) * (d // PEER_KC)))

        def next_gate_tiles(count):
            for _ in range(count):
                if gate_jobs:
                    r, cb = gate_jobs.pop(0)
                    build_gate_tile(gate_nxt, nxt, r, slice(gblk * cb, gblk * (cb + 1)))

        acts = []
        for cs in cols:
            act = None
            for k0 in range(0, d, PEER_KC):
                part = jnp.dot(u_ref[:, k0:k0 + PEER_KC], hnt_ref[k0:k0 + PEER_KC, cs], preferred_element_type=F32)
                act = part if act is None else act + part
                next_gate_tiles(per_piece)
            acts.append(act)
        next_gate_tiles(len(gate_jobs))
        for cs, act in zip(cols, acts):
            ga = gate_cur[:, cs] * _gelu_tanh(act).astype(BF16)
            for n0 in range(0, d, PEER_NC):
                acc_ref[cs, n0:n0 + PEER_NC] += lax.dot_general(
                    ga, v_ref[:, n0:n0 + PEER_NC], (((0,), (0,)), ((), ())), preferred_element_type=F32)

    @pl.when(j % 2 == 0)
    def _():
        step(gate_even, gate_odd)

    @pl.when(j % 2 == 1)
    def _():
        step(gate_odd, gate_even)

    @pl.when(j == nj - 1)
    def _():
        cp = pltpu.make_async_copy(acc_ref, o_hbm.at[tok], sem)
        cp.start()
        cp.wait()


def _peer_experts(hnt, h, eu, ev, a1, jt, a2, r2, *, tm=512, te=1024):
    d, t = hnt.shape
    n_heads, n_keys, _ = a1.shape
    n_exp = eu.shape[0]
    tm = min(tm, t)
    rows = te // n_keys
    once = pl.Buffered(1)
    tok = lambda i, j: (0, 0, i)
    return pl.pallas_call(
        functools.partial(_peer_expert_kernel, n_heads=n_heads, rows=rows),
        grid=(t // tm, n_exp // te),
        in_specs=[
            pl.BlockSpec((d, tm), lambda i, j: (0, i), pipeline_mode=once),
            pl.BlockSpec(memory_space=pl.ANY),
            pl.BlockSpec((te, d), lambda i, j: (j, 0)),
            pl.BlockSpec((te, d), lambda i, j: (j, 0)),
            pl.BlockSpec((n_heads, n_keys, tm), tok, pipeline_mode=once),
            pl.BlockSpec((n_heads, n_keys, tm), tok, pipeline_mode=once),
            pl.BlockSpec((n_heads, n_keys, tm), tok, pipeline_mode=once),
            pl.BlockSpec((n_heads, n_keys, tm), tok, pipeline_mode=once),
        ],
        out_specs=pl.BlockSpec(memory_space=pl.ANY),
        out_shape=jax.ShapeDtypeStruct((t, d), F32),
        scratch_shapes=[pltpu.VMEM((tm, d), F32), pltpu.VMEM((te, tm), BF16), pltpu.VMEM((te, tm), BF16),
                        pltpu.SemaphoreType.DMA(())],
        compiler_params=_params(("arbitrary", "arbitrary")),
        name="peer_experts",
    )(hnt, h, eu, ev, a1, jt, a2, r2)


def _layer(h, pos_col, p, *, batch, seq):
    t, d = h.shape
    n_mla = p["w_uq"].shape[1] // QK_HEAD_DIM
    q_lora, kv_lora = p["w_uq"].shape[0], p["w_ukv"].shape[0]
    n_ssm = p["a_log_fwd"].shape[0]
    ssm_width = n_ssm * SSM_HEAD_DIM
    conv_dim = p["conv_b"].shape[0]
    n_groups = (conv_dim - ssm_width) // (2 * D_STATE)
    hpg = n_ssm // n_groups

    sizes = (q_lora, kv_lora, QK_ROPE_DIM, ssm_width, conv_dim, n_ssm, n_ssm)
    offs = [0]
    for s_ in sizes:
        offs.append(offs[-1] + s_)
    small_pad = LANES - QK_ROPE_DIM - 2 * n_ssm

    def plan(order):
        moves, dst = [], 0
        for k in order:
            moves.append((offs[k], dst, sizes[k]))
            dst += sizes[k]
        return tuple(moves), dst

    main_plan, main_width = plan((4, 3, 0, 1))
    small_plan, _ = plan((2, 5, 6))
    w_main, w_small = _relayout_cast(p["w_in"], (main_plan, small_plan), (main_width, LANES), BF16)
    z_blk = conv_dim // ssm_width
    cq_blk = (conv_dim + ssm_width) // q_lora
    ckv_blk = (conv_dim + ssm_width + q_lora) // kv_lora

    wq = p["w_uq"].reshape(q_lora, n_mla, QK_HEAD_DIM)
    wq = jnp.pad(wq, ((0, 0), (0, 0), (0, QK_PAD_DIM - QK_HEAD_DIM))).reshape(q_lora, n_mla * QK_PAD_DIM).astype(BF16)
    wkv = p["w_ukv"].reshape(kv_lora, n_mla, 2, QK_NOPE_DIM).transpose(0, 2, 1, 3).reshape(kv_lora, -1).astype(BF16)
    pad_gain = lambda g: jnp.pad(g, (0, QK_PAD_DIM - QK_HEAD_DIM)).reshape(1, QK_PAD_DIM)
    half = QK_ROPE_DIM // 2
    invf = ROPE_BASE ** (-jnp.arange(half, dtype=F32) / half)
    invf = jnp.concatenate([invf, invf, jnp.zeros((LANES - QK_ROPE_DIM,), F32)]).reshape(1, LANES)

    def head_vec(fwd, bwd):
        v = jnp.concatenate([jnp.zeros((QK_ROPE_DIM,), F32), fwd, bwd, jnp.zeros((small_pad,), F32)])
        return v.reshape(1, LANES), v.reshape(LANES, 1)

    bias_l, bias_c = head_vec(p["dt_bias_fwd"], p["dt_bias_bwd"])
    alog_l, alog_c = head_vec(p["a_log_fwd"], p["a_log_bwd"])

    xn = _rmsnorm(h, p["norm_mix"], BF16)
    proj = _matmul([xn], [w_main], BF16, tn=1536, name="in_proj")
    small = _matmul([xn], [w_small], F32, name="in_proj_small")
    small_t = small.T

    q, k, v = _mla_prep(proj, small, pos_col, invf, p["q_a_norm"].reshape(1, -1), p["kv_a_norm"].reshape(1, -1),
                        pad_gain(p["q_norm"]), pad_gain(p["k_norm"]), wq, wkv,
                        cq_blk=cq_blk, ckv_blk=ckv_blk, n_heads=n_mla)
    attn = _attention(q, k, v, p["attn_out_norm"].reshape(n_mla, 1, V_HEAD_DIM), batch=batch, seq=seq, n_heads=n_mla)

    cw = p["conv_w"].reshape(D_CONV, conv_dim)
    cb = p["conv_b"].reshape(1, conv_dim)
    ssd_kw = dict(batch=batch, seq=seq, n_heads=n_ssm, heads_per_group=hpg, conv_dim=conv_dim)
    y_fwd, xbc = _ssd_pass(proj, small, small_t, cw, cb, bias_l, alog_l, bias_c, alog_c, reverse=False, **ssd_kw)
    d_skip = jnp.repeat(p["d_skip"], SSM_HEAD_DIM).reshape(1, ssm_width)
    ssm = _ssd_pass(proj, small, small_t, cw, cb, bias_l, alog_l, bias_c, alog_c, reverse=True,
                    final_args=(xbc, y_fwd, d_skip, p["ssm_out_norm"].reshape(1, ssm_width)), z_blk=z_blk,
                    **ssd_kw)

    w_out = p["w_out"].astype(BF16)
    assert n_mla * V_HEAD_DIM == ssm_width, "output projection assumes two equal-width mixer halves"
    h = _matmul([attn, ssm], [(w_out, 0), (w_out, 1)], F32, addend=h, tn=1024, name="out_proj")

    hn, hnt = _rmsnorm(h, p["norm_ffn"], BF16, with_transpose=True)
    pq = _matmul([hn], [p["w_query"].astype(BF16)], BF16, tn=1024, name="peer_query")
    a1, jt, a2, r2 = _peer_topk(pq, p["sub_keys"].astype(BF16))
    return _peer_experts(hnt, h, p["expert_u"].astype(BF16), p["expert_v"].astype(BF16), a1, jt, a2, r2)


_PARAM_NAMES = ("norm_mix", "w_in", "q_a_norm", "w_uq", "kv_a_norm", "w_ukv", "q_norm", "k_norm", "attn_out_norm",
                "conv_w", "conv_b", "a_log_fwd", "a_log_bwd", "dt_bias_fwd", "dt_bias_bwd", "d_skip",
                "ssm_out_norm", "w_out", "norm_ffn", "w_query", "sub_keys", "expert_u", "expert_v")


def kernel(x, positions, norm_mix, w_in, q_a_norm, w_uq, kv_a_norm, w_ukv, q_norm, k_norm, attn_out_norm, conv_w, conv_b, a_log_fwd, a_log_bwd, dt_bias_fwd, dt_bias_bwd, d_skip, ssm_out_norm, w_out, norm_ffn, w_query, sub_keys, expert_u, expert_v):
    weights = (norm_mix, w_in, q_a_norm, w_uq, kv_a_norm, w_ukv, q_norm, k_norm, attn_out_norm, conv_w, conv_b,
               a_log_fwd, a_log_bwd, dt_bias_fwd, dt_bias_bwd, d_skip, ssm_out_norm, w_out, norm_ffn, w_query,
               sub_keys, expert_u, expert_v)
    batch, seq, d = x.shape
    h = x.reshape(batch * seq, d)
    pos_col = positions.reshape(batch * seq, 1).astype(F32)
    for layer in range(norm_mix.shape[0]):
        p = {name: w[layer] for name, w in zip(_PARAM_NAMES, weights)}
        h = _layer(h, pos_col, p, batch=batch, seq=seq)
    return h.reshape(batch, seq, d).astype(x.dtype)
```

```python
import functools

import jax
import jax.numpy as jnp
from jax import lax
from jax.experimental import pallas as pl
from jax.experimental.pallas import tpu as pltpu

F32 = jnp.float32
BF16 = jnp.bfloat16

EPS = 1e-6
ROPE_BASE = 10000.0
QK_NOPE_DIM = 128
QK_ROPE_DIM = 64
QK_HEAD_DIM = QK_NOPE_DIM + QK_ROPE_DIM
QK_PAD_DIM = 256
V_HEAD_DIM = 128
SSM_HEAD_DIM = 64
D_STATE = 128
D_CONV = 5
CHUNK = 128
PEER_TOPK = 16
LANES = 128
HALO = 16
VMEM_LIMIT = 56 * 1024 * 1024
LOG2_E = 1.4426950408889634


def _params(sem, vmem=VMEM_LIMIT):
    return pltpu.CompilerParams(dimension_semantics=sem, vmem_limit_bytes=vmem)


def _rmsnorm_kernel(x_ref, g_ref, o_ref, *t_refs):
    x = x_ref[...].astype(F32)
    ms = jnp.mean(x * x, axis=-1, keepdims=True)
    y = x * lax.rsqrt(ms + EPS) * g_ref[...]
    o_ref[...] = y.astype(o_ref.dtype)
    for t_ref in t_refs:
        t_ref[...] = y.T.astype(t_ref.dtype)


def _rmsnorm(x, gain, out_dtype, tm=256, with_transpose=False):
    t, d = x.shape
    tm = min(tm, t)
    out_specs = [pl.BlockSpec((tm, d), lambda i: (i, 0))]
    out_shape = [jax.ShapeDtypeStruct((t, d), out_dtype)]
    if with_transpose:
        out_specs.append(pl.BlockSpec((d, tm), lambda i: (0, i)))
        out_shape.append(jax.ShapeDtypeStruct((d, t), out_dtype))
    out = pl.pallas_call(
        _rmsnorm_kernel,
        grid=(t // tm,),
        in_specs=[pl.BlockSpec((tm, d), lambda i: (i, 0)), pl.BlockSpec((1, d), lambda i: (0, 0))],
        out_specs=out_specs,
        out_shape=out_shape,
        compiler_params=_params(("parallel",)),
        name="rmsnorm_t" if with_transpose else "rmsnorm",
    )(x, gain.reshape(1, d).astype(F32))
    return out if with_transpose else out[0]


def _matmul_kernel(*refs, n_pairs, has_add):
    o_ref = refs[-1]
    acc = None
    for p in range(n_pairs):
        part = jnp.dot(refs[p][...], refs[n_pairs + p][...], preferred_element_type=F32)
        acc = part if acc is None else acc + part
    if has_add:
        acc = acc + refs[2 * n_pairs][...].astype(F32)
    o_ref[...] = acc.astype(o_ref.dtype)


def _matmul(a_list, b_list, out_dtype, addend=None, tm=1024, tn=512, name="matmul"):
    m = a_list[0].shape[0]
    b_list = [b if isinstance(b, tuple) else (b, 0) for b in b_list]
    n = b_list[0][0].shape[1]
    tm = min(tm, m)
    tn = min(tn, n)
    while n % tn:
        tn -= LANES
    in_specs = [pl.BlockSpec((tm, a.shape[1]), lambda i, j: (i, 0)) for a in a_list]
    in_specs += [pl.BlockSpec((a.shape[1], tn), lambda i, j, rb=rb: (rb, j)) for a, (_, rb) in zip(a_list, b_list)]
    args = list(a_list) + [b for b, _ in b_list]
    if addend is not None:
        in_specs.append(pl.BlockSpec((tm, tn), lambda i, j: (i, j)))
        args.append(addend)
    return pl.pallas_call(
        functools.partial(_matmul_kernel, n_pairs=len(a_list), has_add=addend is not None),
        grid=(m // tm, n // tn),
        in_specs=in_specs,
        out_specs=pl.BlockSpec((tm, tn), lambda i, j: (i, j)),
        out_shape=jax.ShapeDtypeStruct((m, n), out_dtype),
        compiler_params=_params(("parallel", "parallel")),
        name=name,
    )(*args)


def _relayout_kernel(src_ref, *out_refs, plans):
    for o_ref, plan in zip(out_refs, plans):
        filled = 0
        for src0, dst0, width in plan:
            o_ref[:, dst0:dst0 + width] = src_ref[:, src0:src0 + width].astype(o_ref.dtype)
            filled = max(filled, dst0 + width)
        if filled < o_ref.shape[1]:
            o_ref[:, filled:] = jnp.zeros((o_ref.shape[0], o_ref.shape[1] - filled), o_ref.dtype)


def _relayout_cast(w, plans, widths, out_dtype, tr=256):
    rows, cols = w.shape
    tr = min(tr, rows)
    return pl.pallas_call(
        functools.partial(_relayout_kernel, plans=plans),
        grid=(rows // tr,),
        in_specs=[pl.BlockSpec((tr, cols), lambda i: (i, 0))],
        out_specs=[pl.BlockSpec((tr, wd), lambda i: (i, 0)) for wd in widths],
        out_shape=[jax.ShapeDtypeStruct((rows, wd), out_dtype) for wd in widths],
        compiler_params=_params(("parallel",)),
        name="w_in_relayout",
    )(w)


def _rope_tile(v, cos_t, sin_lo, sin_hi):
    return v * cos_t + pltpu.roll(v, LANES - 32, 1) * sin_lo + pltpu.roll(v, 32, 1) * sin_hi


def _mla_prep_kernel(cq_ref, ckv_ref, sm_ref, pos_ref, invf_ref, gqa_ref, gkva_ref, gq_ref, gk_ref,
                     wq_ref, wkv_ref, q_ref, k_ref, v_ref, *, n_heads, scale):
    tm = cq_ref.shape[0]
    lane = lax.broadcasted_iota(jnp.int32, (tm, LANES), 1)
    ang = pos_ref[...] * invf_ref[...]
    cos_t = jnp.where(lane < QK_ROPE_DIM, jnp.cos(ang), 0.0)
    sin_a = jnp.sin(ang)
    sin_lo = jnp.where(lane < 32, -sin_a, 0.0)
    sin_hi = jnp.where((lane >= 32) & (lane < QK_ROPE_DIM), sin_a, 0.0)

    def norm_rows(x, g):
        ms = jnp.mean(x * x, axis=-1, keepdims=True)
        return (x * lax.rsqrt(ms + EPS) * g).astype(BF16)

    cq = norm_rows(cq_ref[...].astype(F32), gqa_ref[...])
    ckv = norm_rows(ckv_ref[...].astype(F32), gkva_ref[...])
    qf = jnp.dot(cq, wq_ref[...], preferred_element_type=F32)
    kvf = jnp.dot(ckv, wkv_ref[...], preferred_element_type=F32)

    gq = gq_ref[...]
    gk = gk_ref[...]
    kr = jnp.where(lane < QK_ROPE_DIM, sm_ref[...], 0.0)
    kr_ss = jnp.sum(kr * kr, axis=-1, keepdims=True)
    kr_rot = _rope_tile(kr * gk[:, QK_NOPE_DIM:], cos_t, sin_lo, sin_hi)
    inv_dim = 1.0 / QK_HEAD_DIM
    for h in range(n_heads):
        qn = qf[:, QK_PAD_DIM * h:QK_PAD_DIM * h + QK_NOPE_DIM]
        qr = qf[:, QK_PAD_DIM * h + QK_NOPE_DIM:QK_PAD_DIM * (h + 1)]
        ss = jnp.sum(qn * qn, axis=-1, keepdims=True) + jnp.sum(qr * qr, axis=-1, keepdims=True)
        sc = lax.rsqrt(ss * inv_dim + EPS) * scale
        q_ref[:, QK_PAD_DIM * h:QK_PAD_DIM * h + QK_NOPE_DIM] = (qn * sc * gq[:, :QK_NOPE_DIM]).astype(BF16)
        q_ref[:, QK_PAD_DIM * h + QK_NOPE_DIM:QK_PAD_DIM * (h + 1)] = (
            _rope_tile(qr * gq[:, QK_NOPE_DIM:], cos_t, sin_lo, sin_hi) * sc).astype(BF16)
        kn = kvf[:, QK_NOPE_DIM * h:QK_NOPE_DIM * (h + 1)]
        ssk = jnp.sum(kn * kn, axis=-1, keepdims=True) + kr_ss
        sck = lax.rsqrt(ssk * inv_dim + EPS)
        k_ref[:, QK_PAD_DIM * h:QK_PAD_DIM * h + QK_NOPE_DIM] = (kn * sck * gk[:, :QK_NOPE_DIM]).astype(BF16)
        k_ref[:, QK_PAD_DIM * h + QK_NOPE_DIM:QK_PAD_DIM * (h + 1)] = (kr_rot * sck).astype(BF16)
    v_ref[...] = kvf[:, n_heads * QK_NOPE_DIM:].astype(BF16)


def _mla_prep(proj, small, pos_col, invf, gqa, gkva, gq, gk, wq, wkv, *, cq_blk, ckv_blk, n_heads, tm=256):
    t = proj.shape[0]
    tm = min(tm, t)
    q_lora, kv_lora = wq.shape[0], wkv.shape[0]
    const = lambda i: (0, 0)
    return pl.pallas_call(
        functools.partial(_mla_prep_kernel, n_heads=n_heads, scale=QK_HEAD_DIM ** -0.5 * LOG2_E),
        grid=(t // tm,),
        in_specs=[
            pl.BlockSpec((tm, q_lora), lambda i: (i, cq_blk)),
            pl.BlockSpec((tm, kv_lora), lambda i: (i, ckv_blk)),
            pl.BlockSpec((tm, LANES), lambda i: (i, 0)),
            pl.BlockSpec((tm, 1), lambda i: (i, 0)),
            pl.BlockSpec((1, LANES), const),
            pl.BlockSpec((1, q_lora), const),
            pl.BlockSpec((1, kv_lora), const),
            pl.BlockSpec((1, QK_PAD_DIM), const),
            pl.BlockSpec((1, QK_PAD_DIM), const),
            pl.BlockSpec(wq.shape, const),
            pl.BlockSpec(wkv.shape, const),
        ],
        out_specs=[
            pl.BlockSpec((tm, n_heads * QK_PAD_DIM), lambda i: (i, 0)),
            pl.BlockSpec((tm, n_heads * QK_PAD_DIM), lambda i: (i, 0)),
            pl.BlockSpec((tm, n_heads * V_HEAD_DIM), lambda i: (i, 0)),
        ],
        out_shape=[
            jax.ShapeDtypeStruct((t, n_heads * QK_PAD_DIM), BF16),
            jax.ShapeDtypeStruct((t, n_heads * QK_PAD_DIM), BF16),
            jax.ShapeDtypeStruct((t, n_heads * V_HEAD_DIM), BF16),
        ],
        compiler_params=_params(("parallel",)),
        name="mla_prep",
    )(proj, proj, small, pos_col, invf, gqa, gkva, gq, gk, wq, wkv)


ATTN_SUB = 256


def _attn_kernel(q_ref, k_ref, v_ref, g_ref, o_ref, vaug_s):
    @pl.when(pl.program_id(2) == 0)
    def _():
        vaug_s[:, :V_HEAD_DIM] = v_ref[...]
        vaug_s[:, V_HEAD_DIM:] = jnp.ones((vaug_s.shape[0], V_HEAD_DIM), vaug_s.dtype)

    tq = q_ref.shape[0]
    sub = min(ATTN_SUB, tq)
    for r in range(tq // sub):
        rows = slice(sub * r, sub * (r + 1))
        s = lax.dot_general(q_ref[rows], k_ref[...], (((1,), (1,)), ((), ())), preferred_element_type=F32)
        m = jnp.max(s, axis=-1, keepdims=True)
        p = jnp.exp2(s - m).astype(BF16)
        oa = jnp.dot(p, vaug_s[...], preferred_element_type=F32)
        o = oa[:, :V_HEAD_DIM] / oa[:, V_HEAD_DIM:]
        ms = jnp.mean(o * o, axis=-1, keepdims=True)
        o_ref[rows] = (o * lax.rsqrt(ms + EPS) * g_ref[0]).astype(o_ref.dtype)


def _attention(q, k, v, gain, *, batch, seq, n_heads, tq=2048):
    t = q.shape[0]
    tq = min(tq, seq)
    nq = seq // tq
    return pl.pallas_call(
        _attn_kernel,
        grid=(batch, n_heads, nq),
        in_specs=[
            pl.BlockSpec((tq, QK_PAD_DIM), lambda b, h, i: (b * nq + i, h)),
            pl.BlockSpec((seq, QK_PAD_DIM), lambda b, h, i: (b, h)),
            pl.BlockSpec((seq, V_HEAD_DIM), lambda b, h, i: (b, h)),
            pl.BlockSpec((1, 1, V_HEAD_DIM), lambda b, h, i: (h, 0, 0)),
        ],
        out_specs=pl.BlockSpec((tq, V_HEAD_DIM), lambda b, h, i: (b * nq + i, h)),
        out_shape=jax.ShapeDtypeStruct((t, n_heads * V_HEAD_DIM), BF16),
        scratch_shapes=[pltpu.VMEM((seq, 2 * V_HEAD_DIM), BF16)],
        compiler_params=_params(("parallel", "parallel", "arbitrary")),
        name="attention",
    )(q, k, v, gain)


def _softplus(x):
    return jnp.maximum(x, 0.0) + jnp.log1p(jnp.exp(-jnp.abs(x)))


def _silu(x):
    return x * jax.nn.sigmoid(x)


def _ssd_kernel(*refs, reverse, nc, n_heads, heads_per_group, finalize):
    if finalize:
        (xbc_ref, sm_ref, smt_ref, bl_ref, al_ref, bc_ref, ac_ref,
         z_ref, yf_ref, dsk_ref, gn_ref, o_ref, y_s, state_s) = refs
    else:
        (xc_ref, xp_ref, xn_ref, sm_ref, smt_ref, cw_ref, cb_ref, bl_ref, al_ref, bc_ref, ac_ref,
         o_ref, xbc_ref, ext_s, y_s, state_s) = refs
    step = pl.program_id(1)
    c = (nc - 1 - step) if reverse else step
    width = n_heads * SSM_HEAD_DIM
    n_groups = n_heads // heads_per_group

    @pl.when(step == 0)
    def _():
        state_s[...] = jnp.zeros_like(state_s)

    if not finalize:
        conv_dim = xc_ref.shape[1]
        keep_prev = jnp.where(c == 0, 0.0, 1.0)
        keep_next = jnp.where(c == nc - 1, 0.0, 1.0)
        ext_s[0:8, :] = xp_ref[...].astype(F32)[HALO - 8:] * keep_prev
        ext_s[8:8 + CHUNK, :] = xc_ref[...].astype(F32)
        ext_s[8 + CHUNK:, :] = xn_ref[...].astype(F32)[:8] * keep_next
        cblk = 512 if conv_dim % 512 == 0 else LANES
        for j in range(conv_dim // cblk):
            cs = slice(cblk * j, cblk * (j + 1))
            acc = cb_ref[:, cs]
            for kk in range(D_CONV):
                off = 8 - D_CONV // 2 + kk
                acc = acc + cw_ref[kk:kk + 1, cs] * ext_s[off:off + CHUNK, cs]
            xbc_ref[:, cs] = _silu(acc).astype(xbc_ref.dtype)

    lane0 = QK_ROPE_DIM + (n_heads if reverse else 0)
    io_r = lax.broadcasted_iota(jnp.int32, (CHUNK, CHUNK), 0)
    io_c = lax.broadcasted_iota(jnp.int32, (CHUNK, CHUNK), 1)
    tri = (io_c >= io_r) if reverse else (io_c <= io_r)
    tri_t = (io_r >= io_c) if reverse else (io_r <= io_c)
    dtl = _softplus(sm_ref[...] + bl_ref[...])
    adt = dtl * (-jnp.exp(al_ref[...]))
    cum = jnp.dot(tri.astype(F32), adt, preferred_element_type=F32, precision=lax.Precision.HIGHEST)
    dtt = _softplus(smt_ref[...] + bc_ref[...])
    adtt = dtt * (-jnp.exp(ac_ref[...]))
    cum_t = jnp.dot(adtt, tri_t.astype(F32), preferred_element_type=F32, precision=lax.Precision.HIGHEST)
    end = 0 if reverse else CHUNK - 1
    total = cum[end:end + 1, :]
    to_end = jnp.exp(total - cum)
    from_start = jnp.exp(cum)
    chunk_decay = jnp.exp(total)

    lane = lax.broadcasted_iota(jnp.int32, (CHUNK, LANES), 1)
    lo_half = lane < SSM_HEAD_DIM

    def pair_cols(arr, h0):
        a = jnp.broadcast_to(arr[:, lane0 + h0:lane0 + h0 + 1], (arr.shape[0], LANES))
        b = jnp.broadcast_to(arr[:, lane0 + h0 + 1:lane0 + h0 + 2], (arr.shape[0], LANES))
        return jnp.where(lo_half[:arr.shape[0]], a, b)

    for g in range(n_groups):
        b_g = xbc_ref[:, width + D_STATE * g:width + D_STATE * (g + 1)]
        c_g = xbc_ref[:, width + D_STATE * (n_groups + g):width + D_STATE * (n_groups + g + 1)]
        cb = lax.dot_general(c_g, b_g, (((1,), (1,)), ((), ())), preferred_element_type=F32)
        for pp in range(heads_per_group // 2):
            h0 = g * heads_per_group + 2 * pp
            q = h0 // 2
            lg = []
            for e in range(2):
                col = cum[:, lane0 + h0 + e:lane0 + h0 + e + 1]
                row = cum_t[lane0 + h0 + e:lane0 + h0 + e + 1, :]
                seg = jnp.where(tri, col - row, -jnp.inf)
                lg.append((jnp.exp(seg) * cb).astype(BF16))
            lg = jnp.concatenate(lg, axis=1)
            xd = xbc_ref[:, LANES * q:LANES * (q + 1)].astype(F32) * pair_cols(dtl, h0)
            xbd = jnp.concatenate([jnp.where(lo_half, xd, 0.0), jnp.where(lo_half, 0.0, xd)],
                                  axis=0).astype(BF16)
            y = jnp.dot(lg, xbd, preferred_element_type=F32)
            st = state_s[q]
            y = y + pair_cols(from_start, h0) * jnp.dot(c_g, st.astype(BF16), preferred_element_type=F32)
            xds = (xd * pair_cols(to_end, h0)).astype(BF16)
            state_s[q] = st * pair_cols(chunk_decay, h0) + lax.dot_general(
                b_g, xds, (((0,), (0,)), ((), ())), preferred_element_type=F32)
            y_s[:, LANES * q:LANES * (q + 1)] = y

    if not finalize:
        o_ref[...] = y_s[...]
    else:
        gsz = width // n_groups
        for g in range(n_groups):
            cs = slice(gsz * g, gsz * (g + 1))
            y = y_s[:, cs] + yf_ref[:, cs] + dsk_ref[:, cs] * xbc_ref[:, cs].astype(F32)
            y = y * _silu(z_ref[:, cs].astype(F32))
            ms = jnp.mean(y * y, axis=-1, keepdims=True)
            o_ref[:, cs] = (y * lax.rsqrt(ms + EPS) * gn_ref[:, cs]).astype(o_ref.dtype)


def _ssd_pass(proj, small, small_t, cw, cb, bias_l, alog_l, bias_c, alog_c, *, batch, seq, n_heads,
              heads_per_group, conv_dim, reverse, final_args=None, z_blk=None):
    t = proj.shape[0]
    nc = seq // CHUNK
    width = n_heads * SSM_HEAD_DIM
    hb = CHUNK // HALO
    n_halo = t // HALO

    def chunk_of(b, s):
        return b * nc + ((nc - 1 - s) if reverse else s)

    const = lambda b, s: (0, 0)
    chunk_rows = lambda b, s: (chunk_of(b, s), 0)
    scan_specs = [
        pl.BlockSpec((CHUNK, LANES), chunk_rows),
        pl.BlockSpec((LANES, CHUNK), lambda b, s: (0, chunk_of(b, s))),
    ]
    head_specs = [
        pl.BlockSpec((1, LANES), const),
        pl.BlockSpec((1, LANES), const),
        pl.BlockSpec((LANES, 1), const),
        pl.BlockSpec((LANES, 1), const),
    ]
    y_spec = pl.BlockSpec((CHUNK, width), chunk_rows)
    scratch = [pltpu.VMEM((CHUNK, width), F32), pltpu.VMEM((n_heads // 2, D_STATE, LANES), F32)]
    finalize = final_args is not None
    if finalize:
        xbc, y_fwd, d_skip, gain = final_args
        in_specs = [pl.BlockSpec((CHUNK, conv_dim), chunk_rows)] + scan_specs + head_specs + [
            pl.BlockSpec((CHUNK, width), lambda b, s: (chunk_of(b, s), z_blk)),
            y_spec,
            pl.BlockSpec((1, width), const),
            pl.BlockSpec((1, width), const),
        ]
        args = [xbc, small, small_t, bias_l, alog_l, bias_c, alog_c, proj, y_fwd, d_skip, gain]
        out_specs = y_spec
        out_shape = jax.ShapeDtypeStruct((t, width), BF16)
    else:
        in_specs = [
            pl.BlockSpec((CHUNK, conv_dim), chunk_rows),
            pl.BlockSpec((HALO, conv_dim), lambda b, s: (jnp.maximum(chunk_of(b, s) * hb - 1, 0), 0)),
            pl.BlockSpec((HALO, conv_dim), lambda b, s: (jnp.minimum((chunk_of(b, s) + 1) * hb, n_halo - 1), 0)),
        ] + scan_specs + [
            pl.BlockSpec((D_CONV, conv_dim), const),
            pl.BlockSpec((1, conv_dim), const),
        ] + head_specs
        args = [proj, proj, proj, small, small_t, cw, cb, bias_l, alog_l, bias_c, alog_c]
        out_specs = [y_spec, pl.BlockSpec((CHUNK, conv_dim), chunk_rows)]
        out_shape = [jax.ShapeDtypeStruct((t, width), F32), jax.ShapeDtypeStruct((t, conv_dim), BF16)]
        scratch = [pltpu.VMEM((CHUNK + 16, conv_dim), F32)] + scratch
    return pl.pallas_call(
        functools.partial(_ssd_kernel, reverse=reverse, nc=nc, n_heads=n_heads,
                          heads_per_group=heads_per_group, finalize=finalize),
        grid=(batch, nc),
        in_specs=in_specs,
        out_specs=out_specs,
        out_shape=out_shape,
        scratch_shapes=scratch,
        compiler_params=_params(("parallel", "arbitrary")),
        name="ssd_bwd" if reverse else "ssd_fwd",
    )(*args)


def _top16(s, tie_safe):
    rows, n = s.shape
    io = lax.broadcasted_iota(jnp.int32, s.shape, 0).astype(F32)
    io16 = lax.broadcasted_iota(jnp.int32, (PEER_TOPK, n), 0)
    rank = jnp.full(s.shape, float(PEER_TOPK), F32)
    vals = jnp.zeros((PEER_TOPK, n), F32)
    for i in range(PEER_TOPK):
        m = jnp.max(s, axis=0, keepdims=True)
        hit = s == m
        if tie_safe:
            hit = io == jnp.min(jnp.where(hit, io, float(rows)), axis=0, keepdims=True)
        rank = jnp.where(hit, float(i), rank)
        s = jnp.where(hit, -jnp.inf, s)
        vals = jnp.where(io16 == i, m, vals)
    return vals, rank


def _peer_gate_tables(s1, s2, tie_safe):
    v1, rank1 = _top16(s1, tie_safe)
    v2, rank2 = _top16(s2, tie_safe)
    n = v1.shape[1]
    sub8 = lax.broadcasted_iota(jnp.int32, (8, n), 0)
    blocks = [v1[0:1] + v2]
    for i in range(1, 8):
        blocks.append(jnp.where(sub8 < PEER_TOPK // (i + 1), v1[i:i + 1] + v2[0:8], -jnp.inf))
    blocks.append(v1[8:PEER_TOPK] + v2[0:1])
    _, crank = _top16(jnp.concatenate(blocks, axis=0), tie_safe)
    sel = jnp.where(crank < PEER_TOPK, 1.0, 0.0)
    in1 = rank1 < PEER_TOPK
    in2 = rank2 < PEER_TOPK
    counts = [jnp.sum(jnp.where(m, 1.0, 0.0), axis=0, keepdims=True) for m in (in1, in2)]
    counts.append(jnp.sum(sel, axis=0, keepdims=True))
    bad = sum(jnp.sum(jnp.where(c != float(PEER_TOPK), 1.0, 0.0)) for c in counts)
    e1 = jnp.exp(v1 - v1[0:1])
    e2 = jnp.exp(v2 - v2[0:1])
    tail = sel[PEER_TOPK + 56:]
    z = jnp.sum(sel[0:PEER_TOPK] * e2, axis=0, keepdims=True) + jnp.sum(tail * e1[8:], axis=0, keepdims=True)
    jt = jnp.where(rank1 == 0.0, jnp.sum(sel[0:PEER_TOPK], axis=0, keepdims=True), 0.0)
    for i in range(1, PEER_TOPK):
        if i < 8:
            sel_i = sel[PEER_TOPK + 8 * (i - 1):PEER_TOPK + 8 * i]
            count = jnp.sum(sel_i, axis=0, keepdims=True)
            z = z + e1[i:i + 1] * jnp.sum(sel_i * e2[0:8], axis=0, keepdims=True)
        else:
            count = tail[i - 8:i - 7]
        jt = jnp.where(rank1 == float(i), count, jt)
    a1 = jnp.where(in1, jnp.exp(s1 - v1[0:1]), 0.0)
    a2 = jnp.where(in2, jnp.exp(s2 - v2[0:1]) / z, 0.0)
    return (a1, jt, a2, rank2), bad


def _peer_topk_kernel(q_ref, keys_ref, a1_ref, jt_ref, a2_ref, r2_ref):
    hps = keys_ref.shape[0]
    half = q_ref.shape[1] // (2 * hps)
    nt = (((1,), (1,)), ((), ()))
    scores = []
    for g in range(hps):
        c0 = 2 * g * half
        scores.append((
            lax.dot_general(keys_ref[g, 0], q_ref[:, c0:c0 + half], nt, preferred_element_type=F32),
            lax.dot_general(keys_ref[g, 1], q_ref[:, c0 + half:c0 + 2 * half], nt, preferred_element_type=F32)))

    def emit(g, tables):
        for ref, val in zip((a1_ref, jt_ref, a2_ref, r2_ref), tables):
            ref[g] = val.astype(ref.dtype)

    bad = 0.0
    for g, (s1, s2) in enumerate(scores):
        tables, bad_g = _peer_gate_tables(s1, s2, tie_safe=False)
        emit(g, tables)
        bad = bad + bad_g

    @pl.when(bad > 0.0)
    def _():
        for g, (s1, s2) in enumerate(scores):
            emit(g, _peer_gate_tables(s1, s2, tie_safe=True)[0])


def _peer_topk(q, keys, *, tm=256, hps=2):
    t = q.shape[0]
    n_heads, _, n_keys, half = keys.shape
    tm = min(tm, t)
    hps = hps if n_heads % hps == 0 else 1
    out_spec = pl.BlockSpec((hps, n_keys, tm), lambda i, h: (h, 0, i))
    out_sds = [jax.ShapeDtypeStruct((n_heads, n_keys, t), dt) for dt in (F32, F32, BF16, BF16)]
    return pl.pallas_call(
        _peer_topk_kernel,
        grid=(t // tm, n_heads // hps),
        in_specs=[
            pl.BlockSpec((tm, 2 * half * hps), lambda i, h: (i, h)),
            pl.BlockSpec((hps, 2, n_keys, half), lambda i, h: (h, 0, 0, 0)),
        ],
        out_specs=[out_spec] * 4,
        out_shape=out_sds,
        compiler_params=_params(("parallel", "parallel")),
        name="peer_topk",
    )(q, keys)


def _gelu_tanh(x):
    return 0.5 * x * (1.0 + jnp.tanh(0.7978845608028654 * (x + 0.044715 * (x * x * x))))


PEER_SUB = 256
PEER_GATE_LANES = 256
PEER_KC = 256
PEER_NC = 512


def _peer_expert_kernel(hnt_ref, h_hbm, u_ref, v_ref, a1_ref, jt_ref, a2_ref, r2_ref, o_hbm, acc_ref,
                        gate_even, gate_odd, sem, *, n_heads, rows):
    i = pl.program_id(0)
    j = pl.program_id(1)
    nj = pl.num_programs(1)
    tm = acc_ref.shape[0]
    d = u_ref.shape[1]
    n_keys = r2_ref.shape[1]
    tok = pl.ds(pl.multiple_of(i * tm, tm), tm)

    def build_gate_tile(gate_ref, tile, r, ts):
        e1 = tile * rows + r
        g = None
        for h in range(n_heads):
            thr = jt_ref[h, pl.ds(e1, 1), :][:, ts].astype(BF16)
            a1 = a1_ref[h, pl.ds(e1, 1), :][:, ts].astype(BF16)
            term = jnp.where(r2_ref[h, :, ts] < thr, a2_ref[h, :, ts], jnp.zeros((), BF16)) * a1
            g = term if g is None else g + term
        gate_ref[n_keys * r:n_keys * (r + 1), ts] = g

    @pl.when(j == 0)
    def _():
        cp = pltpu.make_async_copy(h_hbm.at[tok], acc_ref, sem)
        cp.start()
        for r in range(rows):
            build_gate_tile(gate_even, 0, r, slice(None))
        cp.wait()

    def step(gate_cur, gate_nxt):
        sub = min(PEER_SUB, tm)
        cols = [slice(sub * c, sub * (c + 1)) for c in range(tm // sub)]
        nxt = (j + 1) % nj
        gblk = min(PEER_GATE_LANES, tm)
        gate_jobs = [(r, cb) for r in range(rows) for cb in range(tm // gblk)]
        per_piece = -(-len(gate_jobs) // (len(cols) * (d // PEER_KC)))

        def next_gate_tiles(count):
            for _ in range(count):
                if gate_jobs:
                    r, cb = gate_jobs.pop(0)
                    build_gate_tile(gate_nxt, nxt, r, slice(gblk * cb, gblk * (cb + 1)))

        acts = []
        for cs in cols:
            act = None
            for k0 in range(0, d, PEER_KC):
                part = jnp.dot(u_ref[:, k0:k0 + PEER_KC], hnt_ref[k0:k0 + PEER_KC, cs], preferred_element_type=F32)
                act = part if act is None else act + part
                next_gate_tiles(per_piece)
            acts.append(act)
        next_gate_tiles(len(gate_jobs))
        for cs, act in zip(cols, acts):
            ga = gate_cur[:, cs] * _gelu_tanh(act).astype(BF16)
            for n0 in range(0, d, PEER_NC):
                acc_ref[cs, n0:n0 + PEER_NC] += lax.dot_general(
                    ga, v_ref[:, n0:n0 + PEER_NC], (((0,), (0,)), ((), ())), preferred_element_type=F32)

    @pl.when(j % 2 == 0)
    def _():
        step(gate_even, gate_odd)

    @pl.when(j % 2 == 1)
    def _():
        step(gate_odd, gate_even)

    @pl.when(j == nj - 1)
    def _():
        cp = pltpu.make_async_copy(acc_ref, o_hbm.at[tok], sem)
        cp.start()
        cp.wait()


def _peer_experts(hnt, h, eu, ev, a1, jt, a2, r2, *, tm=512, te=1024):
    d, t = hnt.shape
    n_heads, n_keys, _ = a1.shape
    n_exp = eu.shape[0]
    tm = min(tm, t)
    rows = te // n_keys
    once = pl.Buffered(1)
    tok = lambda i, j: (0, 0, i)
    return pl.pallas_call(
        functools.partial(_peer_expert_kernel, n_heads=n_heads, rows=rows),
        grid=(t // tm, n_exp // te),
        in_specs=[
            pl.BlockSpec((d, tm), lambda i, j: (0, i), pipeline_mode=once),
            pl.BlockSpec(memory_space=pl.ANY),
            pl.BlockSpec((te, d), lambda i, j: (j, 0)),
            pl.BlockSpec((te, d), lambda i, j: (j, 0)),
            pl.BlockSpec((n_heads, n_keys, tm), tok, pipeline_mode=once),
            pl.BlockSpec((n_heads, n_keys, tm), tok, pipeline_mode=once),
            pl.BlockSpec((n_heads, n_keys, tm), tok, pipeline_mode=once),
            pl.BlockSpec((n_heads, n_keys, tm), tok, pipeline_mode=once),
        ],
        out_specs=pl.BlockSpec(memory_space=pl.ANY),
        out_shape=jax.ShapeDtypeStruct((t, d), F32),
        scratch_shapes=[pltpu.VMEM((tm, d), F32), pltpu.VMEM((te, tm), BF16), pltpu.VMEM((te, tm), BF16),
                        pltpu.SemaphoreType.DMA(())],
        compiler_params=_params(("arbitrary", "arbitrary")),
        name="peer_experts",
    )(hnt, h, eu, ev, a1, jt, a2, r2)


def _layer(h, pos_col, p, *, batch, seq):
    t, d = h.shape
    n_mla = p["w_uq"].shape[1] // QK_HEAD_DIM
    q_lora, kv_lora = p["w_uq"].shape[0], p["w_ukv"].shape[0]
    n_ssm = p["a_log_fwd"].shape[0]
    ssm_width = n_ssm * SSM_HEAD_DIM
    conv_dim = p["conv_b"].shape[0]
    n_groups = (conv_dim - ssm_width) // (2 * D_STATE)
    hpg = n_ssm // n_groups

    sizes = (q_lora, kv_lora, QK_ROPE_DIM, ssm_width, conv_dim, n_ssm, n_ssm)
    offs = [0]
    for s_ in sizes:
        offs.append(offs[-1] + s_)
    small_pad = LANES - QK_ROPE_DIM - 2 * n_ssm

    def plan(order):
        moves, dst = [], 0
        for k in order:
            moves.append((offs[k], dst, sizes[k]))
            dst += sizes[k]
        return tuple(moves), dst

    main_plan, main_width = plan((4, 3, 0, 1))
    small_plan, _ = plan((2, 5, 6))
    w_main, w_small = _relayout_cast(p["w_in"], (main_plan, small_plan), (main_width, LANES), BF16)
    z_blk = conv_dim // ssm_width
    cq_blk = (conv_dim + ssm_width) // q_lora
    ckv_blk = (conv_dim + ssm_width + q_lora) // kv_lora

    wq = p["w_uq"].reshape(q_lora, n_mla, QK_HEAD_DIM)
    wq = jnp.pad(wq, ((0, 0), (0, 0), (0, QK_PAD_DIM - QK_HEAD_DIM))).reshape(q_lora, n_mla * QK_PAD_DIM).astype(BF16)
    wkv = p["w_ukv"].reshape(kv_lora, n_mla, 2, QK_NOPE_DIM).transpose(0, 2, 1, 3).reshape(kv_lora, -1).astype(BF16)
    pad_gain = lambda g: jnp.pad(g, (0, QK_PAD_DIM - QK_HEAD_DIM)).reshape(1, QK_PAD_DIM)
    half = QK_ROPE_DIM // 2
    invf = ROPE_BASE ** (-jnp.arange(half, dtype=F32) / half)
    invf = jnp.concatenate([invf, invf, jnp.zeros((LANES - QK_ROPE_DIM,), F32)]).reshape(1, LANES)

    def head_vec(fwd, bwd):
        v = jnp.concatenate([jnp.zeros((QK_ROPE_DIM,), F32), fwd, bwd, jnp.zeros((small_pad,), F32)])
        return v.reshape(1, LANES), v.reshape(LANES, 1)

    bias_l, bias_c = head_vec(p["dt_bias_fwd"], p["dt_bias_bwd"])
    alog_l, alog_c = head_vec(p["a_log_fwd"], p["a_log_bwd"])

    xn = _rmsnorm(h, p["norm_mix"], BF16)
    proj = _matmul([xn], [w_main], BF16, tn=1536, name="in_proj")
    small = _matmul([xn], [w_small], F32, name="in_proj_small")
    small_t = small.T

    q, k, v = _mla_prep(proj, small, pos_col, invf, p["q_a_norm"].reshape(1, -1), p["kv_a_norm"].reshape(1, -1),
                        pad_gain(p["q_norm"]), pad_gain(p["k_norm"]), wq, wkv,
                        cq_blk=cq_blk, ckv_blk=ckv_blk, n_heads=n_mla)
    attn = _attention(q, k, v, p["attn_out_norm"].reshape(n_mla, 1, V_HEAD_DIM), batch=batch, seq=seq, n_heads=n_mla)

    cw = p["conv_w"].reshape(D_CONV, conv_dim)
    cb = p["conv_b"].reshape(1, conv_dim)
    ssd_kw = dict(batch=batch, seq=seq, n_heads=n_ssm, heads_per_group=hpg, conv_dim=conv_dim)
    y_fwd, xbc = _ssd_pass(proj, small, small_t, cw, cb, bias_l, alog_l, bias_c, alog_c, reverse=False, **ssd_kw)
    d_skip = jnp.repeat(p["d_skip"], SSM_HEAD_DIM).reshape(1, ssm_width)
    ssm = _ssd_pass(proj, small, small_t, cw, cb, bias_l, alog_l, bias_c, alog_c, reverse=True,
                    final_args=(xbc, y_fwd, d_skip, p["ssm_out_norm"].reshape(1, ssm_width)), z_blk=z_blk,
                    **ssd_kw)

    w_out = p["w_out"].astype(BF16)
    assert n_mla * V_HEAD_DIM == ssm_width, "output projection assumes two equal-width mixer halves"
    h = _matmul([attn, ssm], [(w_out, 0), (w_out, 1)], F32, addend=h, tn=1024, name="out_proj")

    hn, hnt = _rmsnorm(h, p["norm_ffn"], BF16, with_transpose=True)
    pq = _matmul([hn], [p["w_query"].astype(BF16)], BF16, tn=1024, name="peer_query")
    a1, jt, a2, r2 = _peer_topk(pq, p["sub_keys"].astype(BF16))
    return _peer_experts(hnt, h, p["expert_u"].astype(BF16), p["expert_v"].astype(BF16), a1, jt, a2, r2)


_PARAM_NAMES = ("norm_mix", "w_in", "q_a_norm", "w_uq", "kv_a_norm", "w_ukv", "q_norm", "k_norm", "attn_out_norm",
                "conv_w", "conv_b", "a_log_fwd", "a_log_bwd", "dt_bias_fwd", "dt_bias_bwd", "d_skip",
                "ssm_out_norm", "w_out", "norm_ffn", "w_query", "sub_keys", "expert_u", "expert_v")


def kernel(x, positions, norm_mix, w_in, q_a_norm, w_uq, kv_a_norm, w_ukv, q_norm, k_norm, attn_out_norm, conv_w, conv_b, a_log_fwd, a_log_bwd, dt_bias_fwd, dt_bias_bwd, d_skip, ssm_out_norm, w_out, norm_ffn, w_query, sub_keys, expert_u, expert_v):
    weights = (norm_mix, w_in, q_a_norm, w_uq, kv_a_norm, w_ukv, q_norm, k_norm, attn_out_norm, conv_w, conv_b,
               a_log_fwd, a_log_bwd, dt_bias_fwd, dt_bias_bwd, d_skip, ssm_out_norm, w_out, norm_ffn, w_query,
               sub_keys, expert_u, expert_v)
    batch, seq, d = x.shape
    h = x.reshape(batch * seq, d)
    pos_col = positions.reshape(batch * seq, 1).astype(F32)
    for layer in range(norm_mix.shape[0]):
        p = {name: w[layer] for name, w in zip(_PARAM_NAMES, weights)}
        h = _layer(h, pos_col, p, batch=batch, seq=seq)
    return h.reshape(batch, seq, d).astype(x.dtype)
```
